```python
import math
import jax, jax.numpy as jnp
from jax import lax
import numpy as np

D_MODEL = 1024
BATCH = 32
SEQ = 2048
DEPTH = 2

CHUNK = 64
Q_BLOCK = 128
N_A_LAYERS = DEPTH // 2
N_B_LAYERS = DEPTH - N_A_LAYERS
SSM_GROUP = 16
N_GROUPS = D_MODEL // SSM_GROUP
SSM_STATE = 64
HEAD_DIM = 64
N_HEADS = D_MODEL // HEAD_DIM
D_FF = 2816
CONV_W = 3
EPS = 1e-6
DT_MIN = 1e-3
DT_MAX = 1e-1

kernel_name = "yoco_s5_stickbreak_hybrid"


def rms_norm(x, g):
    xf = x.astype(jnp.float32)
    y = xf * lax.rsqrt(jnp.mean(xf * xf, axis=-1, keepdims=True) + EPS)
    return (y * g.astype(jnp.float32)).astype(x.dtype)


def s5_mixer(xn, w_in, lam_re, lam_im, b_re, b_im, c_re, c_im, d_skip, log_dt, w_glu):
    f32 = jnp.float32
    bsz, seq, _ = xn.shape
    u = (xn @ w_in).astype(f32)
    lam = lax.complex(lam_re.astype(f32), lam_im.astype(f32))
    dt = jnp.exp(log_dt.astype(f32))[:, None]
    a_bar = jnp.exp(lam * dt)
    b_mat = lax.complex(b_re.astype(f32), b_im.astype(f32))
    b_bar = ((a_bar - 1.0) / lam)[..., None] * b_mat
    c_mat = lax.complex(c_re.astype(f32), c_im.astype(f32))
    n_chunks = seq // CHUNK
    u_c = u.reshape(bsz, n_chunks, CHUNK, N_GROUPS, SSM_GROUP).transpose(1, 2, 0, 3, 4)
    a_elems = jnp.broadcast_to(a_bar, (CHUNK, 1, N_GROUPS, SSM_STATE))

    def binop(left, right):
        a_l, b_l = left
        a_r, b_r = right
        return a_r * a_l, a_r * b_l + b_r

    def chunk_step(h0, u_t):
        bu = jnp.einsum('tbgh,gph->tbgp', u_t.astype(jnp.complex64), b_bar)
        a_cum, s = lax.associative_scan(binop, (a_elems, bu), axis=0)
        s = s + a_cum * h0[None]
        y = jnp.einsum('tbgp,ghp->tbgh', s, c_mat).real
        return s[-1], y

    h0 = jnp.zeros((bsz, N_GROUPS, SSM_STATE), jnp.complex64)
    _, y = lax.scan(chunk_step, h0, u_c)
    y = y.transpose(2, 0, 1, 3, 4).reshape(bsz, seq, D_MODEL)
    y = y + d_skip.astype(f32) * u
    g = jax.nn.gelu(y).astype(xn.dtype)
    val, gate = jnp.split(g @ w_glu, 2, axis=-1)
    return val * jax.nn.sigmoid(gate)


def shared_kv(h, kv_norm, w_kv, k_norm):
    bsz, seq, _ = h.shape
    k, v = jnp.split(rms_norm(h, kv_norm) @ w_kv, 2, axis=-1)
    k = rms_norm(k.reshape(bsz, seq, N_HEADS, HEAD_DIM), k_norm)
    v = v.reshape(bsz, seq, N_HEADS, HEAD_DIM)
    return k, v


def stick_breaking_attention(q, k, v):
    seq = q.shape[1]
    scale = HEAD_DIM ** -0.5
    outs = []
    for i in range(seq // Q_BLOCK):
        q0 = i * Q_BLOCK
        kv_len = q0 + Q_BLOCK
        z = jnp.einsum('bqhd,bkhd->bhqk', q[:, q0:kv_len], k[:, :kv_len]).astype(jnp.float32) * scale
        q_pos = q0 + jnp.arange(Q_BLOCK)[:, None]
        k_pos = jnp.arange(kv_len)[None, :]
        causal = k_pos < q_pos
        log_beta = jax.nn.log_sigmoid(z)
        log_1m = jnp.where(causal, log_beta - z, 0.0)
        later = lax.cumsum(log_1m, axis=3, reverse=True) - log_1m
        w = jnp.where(causal, jnp.exp(log_beta + later), 0.0)
        outs.append(jnp.einsum('bhqk,bkhd->bqhd', w.astype(v.dtype), v[:, :kv_len]))
    return jnp.concatenate(outs, axis=1)


def sb_mixer(xn, w_q, q_norm, k, v, w_o):
    bsz, seq, _ = xn.shape
    q = rms_norm((xn @ w_q).reshape(bsz, seq, N_HEADS, HEAD_DIM), q_norm)
    o = stick_breaking_attention(q, k, v)
    return o.reshape(bsz, seq, D_MODEL) @ w_o


def conv_ffn(xn, w_up, conv_w, conv_b, w_down):
    seq = xn.shape[1]
    val, gate = jnp.split(xn @ w_up, 2, axis=-1)
    gp = jnp.pad(gate, ((0, 0), (CONV_W - 1, 0), (0, 0)))
    gc = conv_b + conv_w[0] * gp[:, 0:seq]
    for j in range(1, CONV_W):
        gc = gc + conv_w[j] * gp[:, j:j + seq]
    return (jax.nn.silu(gc) * val) @ w_down


def setup_inputs(seed: int = 0) -> dict:
    key = jax.random.key(seed)
    ks = iter(jax.random.split(key, 32))

    def nrm(shape, scale):
        return scale * jax.random.normal(next(ks), shape, jnp.float32)

    def gain(shape):
        return 1.0 + nrm(shape, 0.02)

    d, f = D_MODEL, D_FF
    na, nb = N_A_LAYERS, N_B_LAYERS
    G, P, H = N_GROUPS, SSM_STATE, SSM_GROUP
    n_idx = jnp.arange(P, dtype=jnp.float32)
    x = nrm((BATCH, SEQ, d), 1.0)
    a_norm = gain((na, d))
    a_w_in = nrm((na, d, d), d ** -0.5)
    a_lam_re = -0.5 + nrm((na, G, P), 0.01)
    a_lam_im = math.pi * n_idx + nrm((na, G, P), 0.01)
    a_b_re = nrm((na, G, P, H), (2 * H) ** -0.5)
    a_b_im = nrm((na, G, P, H), (2 * H) ** -0.5)
    a_c_re = nrm((na, G, H, P), P ** -0.5)
    a_c_im = nrm((na, G, H, P), P ** -0.5)
    a_d = nrm((na, d), 1.0)
    a_log_dt = jax.random.uniform(next(ks), (na, G), jnp.float32, math.log(DT_MIN), math.log(DT_MAX))
    a_w_glu = nrm((na, d, 2 * d), d ** -0.5)
    kv_norm = gain((d,))
    w_kv = nrm((d, 2 * d), d ** -0.5)
    k_norm = gain((HEAD_DIM,))
    b_norm = gain((nb, d))
    b_w_q = nrm((nb, d, d), d ** -0.5)
    b_q_norm = gain((nb, HEAD_DIM))
    b_w_o = nrm((nb, d, d), d ** -0.5)
    ffn_norm = gain((DEPTH, d))
    ffn_w_up = nrm((DEPTH, d, 2 * f), d ** -0.5)
    ffn_conv_w = nrm((DEPTH, CONV_W, f), CONV_W ** -0.5)
    ffn_conv_b = nrm((DEPTH, f), 0.02)
    ffn_w_down = nrm((DEPTH, f, d), f ** -0.5)
    return {"x": x, "a_norm": a_norm, "a_w_in": a_w_in, "a_lam_re": a_lam_re, "a_lam_im": a_lam_im,
            "a_b_re": a_b_re, "a_b_im": a_b_im, "a_c_re": a_c_re, "a_c_im": a_c_im, "a_d": a_d,
            "a_log_dt": a_log_dt, "a_w_glu": a_w_glu, "kv_norm": kv_norm, "w_kv": w_kv, "k_norm": k_norm,
            "b_norm": b_norm, "b_w_q": b_w_q, "b_q_norm": b_q_norm, "b_w_o": b_w_o,
            "ffn_norm": ffn_norm, "ffn_w_up": ffn_w_up, "ffn_conv_w": ffn_conv_w,
            "ffn_conv_b": ffn_conv_b, "ffn_w_down": ffn_w_down}


def reference(x, a_norm, a_w_in, a_lam_re, a_lam_im, a_b_re, a_b_im, a_c_re, a_c_im, a_d,
              a_log_dt, a_w_glu, kv_norm, w_kv, k_norm, b_norm, b_w_q, b_q_norm, b_w_o,
              ffn_norm, ffn_w_up, ffn_conv_w, ffn_conv_b, ffn_w_down):
    h = x
    k = None
    v = None
    for layer in range(DEPTH):
        if layer < N_A_LAYERS:
            i = layer
            h = h + s5_mixer(rms_norm(h, a_norm[i]), a_w_in[i], a_lam_re[i], a_lam_im[i],
                             a_b_re[i], a_b_im[i], a_c_re[i], a_c_im[i], a_d[i], a_log_dt[i], a_w_glu[i])
        else:
            j = layer - N_A_LAYERS
            if j == 0:
                k, v = shared_kv(h, kv_norm, w_kv, k_norm)
            h = h + sb_mixer(rms_norm(h, b_norm[j]), b_w_q[j], b_q_norm[j], k, v, b_w_o[j])
        h = h + conv_ffn(rms_norm(h, ffn_norm[layer]), ffn_w_up[layer], ffn_conv_w[layer],
                         ffn_conv_b[layer], ffn_w_down[layer])
    return h
```

```python
import functools
import math

import jax
import jax.numpy as jnp
from jax import lax
from jax.experimental import pallas as pl
from jax.experimental.pallas import tpu as pltpu

EPS = 1e-6
HEAD_DIM = 64
SSM_GROUP = 16
SSM_STATE = 64
CONV_W = 3
SUBLANES = 8
LANES = 128
GROUP_BLOCK = 16
VMEM_LIMIT = 56 * 1024 * 1024

BF16 = jnp.bfloat16
F32 = jnp.float32


def _cparams(*sem):
    return pltpu.CompilerParams(dimension_semantics=sem, vmem_limit_bytes=VMEM_LIMIT)


def _resident(shape):
    nd = len(shape)
    return pl.BlockSpec(shape, lambda *_: (0,) * nd, pipeline_mode=pl.Buffered(1))


def _rms(x):
    return lax.rsqrt(jnp.mean(x * x, axis=-1, keepdims=True) + EPS)


def _dot(a, b):
    return jnp.dot(a, b, preferred_element_type=F32)


def _norm_matmul_kernel(x_ref, g_ref, w_ref, o_ref):
    x = x_ref[...]
    xn = (x * _rms(x) * g_ref[...]).astype(BF16)
    o_ref[...] = _dot(xn, w_ref[...])


def _norm_matmul(x, g, w, tm):
    m, d = x.shape
    n = w.shape[1]
    return pl.pallas_call(
        _norm_matmul_kernel,
        grid=(m // tm,),
        in_specs=[pl.BlockSpec((tm, d), lambda i: (i, 0)),
                  _resident((1, d)),
                  _resident((d, n))],
        out_specs=pl.BlockSpec((tm, n), lambda i: (i, 0)),
        out_shape=jax.ShapeDtypeStruct((m, n), F32),
        compiler_params=_cparams("parallel"),
        name="s5_in_proj",
    )(x, g.reshape(1, d), w)


def _s5_kernel(u_ref, bbd_ref, cbd_ref, are_ref, aim_ref, d_ref, o_ref, s_ref, carry_ref,
               *, steps, n_blocks):
    half = GROUP_BLOCK * SSM_STATE
    cb = GROUP_BLOCK * SSM_GROUP
    chunks = 4

    @pl.when(pl.program_id(1) == 0)
    def _():
        carry_ref[...] = jnp.zeros_like(carry_ref)

    for k in range(n_blocks):
        s_ref[...] = _dot(u_ref[:, k * cb:(k + 1) * cb].astype(BF16), bbd_ref[k])

        for c0 in range(0, half // LANES, chunks):
            re_cols = [(c0 + c) * LANES for c in range(chunks)]
            im_cols = [half + col for col in re_cols]
            st_cols = [k * half + col for col in re_cols]
            a_re = [are_ref[:, pl.ds(col, LANES)] for col in st_cols]
            a_im = [aim_ref[:, pl.ds(col, LANES)] for col in st_cols]

            def step(t, state):
                rows = pl.ds(pl.multiple_of(t * SUBLANES, SUBLANES), SUBLANES)
                new = []
                for c in range(chunks):
                    s_re, s_im = state[2 * c], state[2 * c + 1]
                    n_re = a_re[c] * s_re - a_im[c] * s_im + s_ref[rows, pl.ds(re_cols[c], LANES)]
                    n_im = a_re[c] * s_im + a_im[c] * s_re + s_ref[rows, pl.ds(im_cols[c], LANES)]
                    s_ref[rows, pl.ds(re_cols[c], LANES)] = n_re
                    s_ref[rows, pl.ds(im_cols[c], LANES)] = n_im
                    new += [n_re, n_im]
                return tuple(new)

            init = []
            for c in range(chunks):
                init += [carry_ref[0, :, pl.ds(st_cols[c], LANES)],
                         carry_ref[1, :, pl.ds(st_cols[c], LANES)]]
            final = lax.fori_loop(0, steps, step, tuple(init), unroll=2)
            for c in range(chunks):
                carry_ref[0, :, pl.ds(st_cols[c], LANES)] = final[2 * c]
                carry_ref[1, :, pl.ds(st_cols[c], LANES)] = final[2 * c + 1]

        y = _dot(s_ref[...].astype(BF16), cbd_ref[k])
        y = y + d_ref[:, k * cb:(k + 1) * cb] * u_ref[:, k * cb:(k + 1) * cb]
        o_ref[:, k * cb:(k + 1) * cb] = jax.nn.gelu(y).astype(BF16)


def _s5_core(u, bbd, cbd, a_re, a_im, d_skip, n_bhi, seq, steps):
    m, d = u.shape
    n_blocks = bbd.shape[0]
    rows = steps * SUBLANES
    tiles = seq // steps
    n_state = a_re.shape[1]
    kern = functools.partial(_s5_kernel, steps=steps, n_blocks=n_blocks)
    return pl.pallas_call(
        kern,
        grid=(n_bhi, tiles),
        in_specs=[pl.BlockSpec((rows, d), lambda b, t: (b * tiles + t, 0)),
                  _resident(bbd.shape), _resident(cbd.shape),
                  _resident(a_re.shape), _resident(a_im.shape), _resident((1, d))],
        out_specs=pl.BlockSpec((rows, d), lambda b, t: (b * tiles + t, 0)),
        out_shape=jax.ShapeDtypeStruct((m, d), BF16),
        scratch_shapes=[pltpu.VMEM((rows, 2 * GROUP_BLOCK * SSM_STATE), F32),
                        pltpu.VMEM((2, SUBLANES, n_state), F32)],
        compiler_params=_cparams("parallel", "arbitrary"),
        name="s5_core",
    )(u, bbd, cbd, a_re, a_im, d_skip.reshape(1, d))


def _glu_kernel(g_ref, h_ref, w_ref, o_ref):
    d = h_ref.shape[1]
    vg = _dot(g_ref[...], w_ref[...])
    o_ref[...] = h_ref[...] + vg[:, :d] * jax.nn.sigmoid(vg[:, d:])


def _glu(g, h, w, tm):
    m, d = h.shape
    return pl.pallas_call(
        _glu_kernel,
        grid=(m // tm,),
        in_specs=[pl.BlockSpec((tm, d), lambda i: (i, 0)),
                  pl.BlockSpec((tm, d), lambda i: (i, 0)),
                  _resident(w.shape)],
        out_specs=pl.BlockSpec((tm, d), lambda i: (i, 0)),
        out_shape=jax.ShapeDtypeStruct((m, d), F32),
        compiler_params=_cparams("parallel"),
        name="s5_glu",
    )(g, h, w)


def _ffn_kernel(h_ref, g_ref, wv_ref, wg_ref, cw_ref, cb_ref, wd_ref, o_ref,
                xn_ref, gbuf_ref, carry_ref, *, shift, tiles_per_seq):
    tm = h_ref.shape[0]
    n_chunks, _, fc = wv_ref.shape
    hist = (CONV_W - 1) * shift
    pad = gbuf_ref.shape[0] - tm
    first = pl.program_id(0) % tiles_per_seq == 0

    h = h_ref[...]
    xn_ref[...] = (h * _rms(h) * g_ref[...]).astype(BF16)
    acc = h
    for c in range(n_chunks):
        xn = xn_ref[...]
        val = _dot(xn, wv_ref[c])
        gate = _dot(xn, wg_ref[c])

        @pl.when(first)
        def _():
            gbuf_ref[0:pad, :] = jnp.zeros((pad, fc), F32)

        @pl.when(jnp.logical_not(first))
        def _():
            gbuf_ref[0:pad, :] = carry_ref[c]

        gbuf_ref[pad:pad + tm, :] = gate
        carry_ref[c] = gate[tm - pad:tm, :]
        cols = pl.ds(c * fc, fc)
        gc = cb_ref[:, cols] + cw_ref[CONV_W - 1:CONV_W, cols] * gate
        for j in range(CONV_W - 1):
            off = pad - hist + j * shift
            gc = gc + cw_ref[j:j + 1, cols] * gbuf_ref[off:off + tm, :]
        act = (jax.nn.silu(gc) * val).astype(BF16)
        acc = acc + _dot(act, wd_ref[c])
    o_ref[...] = acc


def _conv_ffn(h, g, wv, wg, conv_w, conv_b, wd, tm, shift, rows_per_seq):
    m, d = h.shape
    n_chunks, _, fc = wv.shape
    f = n_chunks * fc
    pad = max(SUBLANES, (CONV_W - 1) * shift)
    kern = functools.partial(_ffn_kernel, shift=shift, tiles_per_seq=rows_per_seq // tm)
    return pl.pallas_call(
        kern,
        grid=(m // tm,),
        in_specs=[pl.BlockSpec((tm, d), lambda i: (i, 0)),
                  _resident((1, d)),
                  _resident(wv.shape), _resident(wg.shape),
                  _resident((CONV_W, f)), _resident((1, f)),
                  _resident(wd.shape)],
        out_specs=pl.BlockSpec((tm, d), lambda i: (i, 0)),
        out_shape=jax.ShapeDtypeStruct((m, d), F32),
        scratch_shapes=[pltpu.VMEM((tm, d), BF16),
                        pltpu.VMEM((pad + tm, fc), F32),
                        pltpu.VMEM((n_chunks, pad, fc), F32)],
        compiler_params=_cparams("arbitrary"),
        name="conv_ffn",
    )(h, g.reshape(1, d), wv, wg, conv_w, conv_b.reshape(1, f), wd)


def _head_norm(x, gain):
    lo = lax.broadcasted_iota(jnp.int32, (1, LANES), 1) < HEAD_DIM
    sq = x * x
    ss_lo = jnp.sum(jnp.where(lo, sq, 0.0), axis=-1, keepdims=True)
    ss_hi = jnp.sum(jnp.where(lo, 0.0, sq), axis=-1, keepdims=True)
    ms = jnp.where(lo, ss_lo, ss_hi) * (1.0 / HEAD_DIM)
    return x * lax.rsqrt(ms + EPS) * gain


def _qkv_kernel(h_ref, gq_ref, gkv_ref, wq_ref, wkv_ref, qg_ref, kg_ref,
                q_ref, k_ref, v_ref):
    d = h_ref.shape[1]
    h = h_ref[...]
    hn = h * _rms(h)
    q = _dot((hn * gq_ref[...]).astype(BF16), wq_ref[...])
    kv = _dot((hn * gkv_ref[...]).astype(BF16), wkv_ref[...])
    for c in range(0, d, LANES):
        q_ref[:, c:c + LANES] = _head_norm(q[:, c:c + LANES], qg_ref[...]).astype(BF16)
        k_ref[:, c:c + LANES] = _head_norm(kv[:, c:c + LANES], kg_ref[...]).astype(BF16)
    v_ref[...] = kv[:, d:].astype(BF16)


def _qkv(h, gq, gkv, wq, wkv, qg, kg, tm):
    m, d = h.shape
    row = pl.BlockSpec((tm, d), lambda i: (i, 0))
    out = jax.ShapeDtypeStruct((m, d), BF16)
    return pl.pallas_call(
        _qkv_kernel,
        grid=(m // tm,),
        in_specs=[row, _resident((1, d)), _resident((1, d)),
                  _resident(wq.shape), _resident(wkv.shape),
                  _resident((1, LANES)), _resident((1, LANES))],
        out_specs=[row, row, row],
        out_shape=[out, out, out],
        compiler_params=_cparams("parallel"),
        name="qkv_proj",
    )(h, gq.reshape(1, d), gkv.reshape(1, d), wq, wkv, qg, kg)


def _softplus(z):
    return jnp.maximum(z, 0.0) + jnp.log(1.0 + jnp.exp(-jnp.abs(z)))


def _attn_kernel(q_ref, k_ref, v_ref, o_ref):
    blk = q_ref.shape[1]
    i = pl.program_id(2)
    lo = lax.broadcasted_iota(jnp.int32, (1, LANES), 1) < HEAD_DIM
    row = lax.broadcasted_iota(jnp.int32, (blk, blk), 0)
    col = lax.broadcasted_iota(jnp.int32, (blk, blk), 1)
    causal = col < row
    later_keys = jnp.where(row > col, 1.0, 0.0).astype(BF16)
    q2 = q_ref[0]

    def suffix_sums(sp):
        hi = sp.astype(BF16)
        lo_part = (sp - hi.astype(F32)).astype(BF16)
        return _dot(hi, later_keys) + _dot(lo_part, later_keys)

    def scores(qm, j):
        kb = k_ref[0, pl.ds(pl.multiple_of(j * blk, blk), blk), :]
        return lax.dot_general(qm, kb, (((1,), (1,)), ((), ())), preferred_element_type=F32)

    def values(j):
        return v_ref[0, pl.ds(pl.multiple_of(j * blk, blk), blk), :]

    heads = []
    for hd in range(2):
        qm = jnp.where(lo if hd == 0 else jnp.logical_not(lo), q2, jnp.zeros_like(q2))

        z = scores(qm, i)
        sp = jnp.where(causal, _softplus(z), 0.0)
        cum = suffix_sums(sp)
        w = jnp.where(causal, jnp.exp(z - sp - cum), 0.0)
        acc = _dot(w.astype(BF16), values(i))
        carry = cum[:, 0:1] + sp[:, 0:1]

        def body(jj, state):
            acc, carry = state
            j = i - 1 - jj
            z = scores(qm, j)
            sp = _softplus(z)
            cum = suffix_sums(sp)
            w = jnp.exp(z - sp - cum - carry)
            acc = acc + _dot(w.astype(BF16), values(j))
            return acc, carry + cum[:, 0:1] + sp[:, 0:1]

        acc, _ = lax.fori_loop(0, i, body, (acc, carry))
        heads.append(acc)
    o_ref[0] = jnp.where(lo, heads[0], heads[1]).astype(BF16)


def _attention(q, k, v, blk):
    b, seq, d = q.shape
    qspec = pl.BlockSpec((1, blk, LANES), lambda bi, hp, i: (bi, i, hp))
    kvspec = pl.BlockSpec((1, seq, LANES), lambda bi, hp, i: (bi, 0, hp))
    return pl.pallas_call(
        _attn_kernel,
        grid=(b, d // LANES, seq // blk),
        in_specs=[qspec, kvspec, kvspec],
        out_specs=qspec,
        out_shape=jax.ShapeDtypeStruct((b, seq, d), BF16),
        compiler_params=_cparams("parallel", "parallel", "arbitrary"),
        name="sb_attention",
    )(q, k, v)


def _oproj_kernel(o_ref, h_ref, w_ref, out_ref):
    out_ref[...] = h_ref[...] + _dot(o_ref[...], w_ref[...])


def _oproj(o, h, w, tm):
    m, d = h.shape
    row = pl.BlockSpec((tm, d), lambda i: (i, 0))
    return pl.pallas_call(
        _oproj_kernel,
        grid=(m // tm,),
        in_specs=[row, row, _resident(w.shape)],
        out_specs=row,
        out_shape=jax.ShapeDtypeStruct((m, d), F32),
        compiler_params=_cparams("parallel"),
        name="attn_out_proj",
    )(o, h, w)


def _s5_tables(lam_re, lam_im, b_re, b_im, c_re, c_im, log_dt):
    n_groups, n_state = lam_re.shape
    n_ch = b_re.shape[2]
    n_blocks = n_groups // GROUP_BLOCK
    lam = lax.complex(lam_re, lam_im)
    dt = jnp.exp(log_dt)[:, None]
    a_bar = jnp.exp(lam * dt)
    b_bar = ((a_bar - 1.0) / lam)[..., None] * lax.complex(b_re, b_im)
    eye = jnp.eye(GROUP_BLOCK, dtype=F32)
    bb = jnp.stack([b_bar.real, b_bar.imag]).reshape(2, n_blocks, GROUP_BLOCK, n_state, n_ch)
    bbd = jnp.einsum('rkgph,gj->kghrjp', bb, eye).reshape(
        n_blocks, GROUP_BLOCK * n_ch, 2 * GROUP_BLOCK * n_state)
    cc = jnp.stack([c_re, -c_im]).reshape(2, n_blocks, GROUP_BLOCK, n_ch, n_state)
    cbd = jnp.einsum('rkghp,gj->krgpjh', cc, eye).reshape(
        n_blocks, 2 * GROUP_BLOCK * n_state, GROUP_BLOCK * n_ch)
    a_re = jnp.broadcast_to(a_bar.real.reshape(1, -1), (SUBLANES, n_groups * n_state))
    a_im = jnp.broadcast_to(a_bar.imag.reshape(1, -1), (SUBLANES, n_groups * n_state))
    return bbd.astype(BF16), cbd.astype(BF16), a_re, a_im


def _ffn_weights(w_up, w_down, fc):
    d, f2 = w_up.shape
    f = f2 // 2
    n = f // fc
    wv = w_up[:, :f].reshape(d, n, fc).transpose(1, 0, 2).astype(BF16)
    wg = w_up[:, f:].reshape(d, n, fc).transpose(1, 0, 2).astype(BF16)
    wd = w_down.reshape(n, fc, d).astype(BF16)
    return wv, wg, wd


def kernel(x, a_norm, a_w_in, a_lam_re, a_lam_im, a_b_re, a_b_im, a_c_re, a_c_im, a_d, a_log_dt, a_w_glu, kv_norm, w_kv, k_norm, b_norm, b_w_q, b_q_norm, b_w_o, ffn_norm, ffn_w_up, ffn_conv_w, ffn_conv_b, ffn_w_down):
    bsz, seq, d = x.shape
    n_a = a_norm.shape[0]
    depth = ffn_norm.shape[0]
    m = bsz * seq
    n_bhi = bsz // SUBLANES
    tm = 512
    fc = 256
    s5_steps = min(128, seq)
    attn_blk = min(256, seq)
    assert bsz % SUBLANES == 0 and seq % s5_steps == 0 and seq % attn_blk == 0
    assert (seq * SUBLANES) % tm == 0 and seq % tm == 0 and d % LANES == 0

    h = x.reshape(n_bhi, SUBLANES, seq, d).transpose(0, 2, 1, 3).reshape(m, d)
    k = v = None
    natural = False
    for layer in range(depth):
        if layer < n_a:
            i = layer
            bbd, cbd, a_re, a_im = _s5_tables(a_lam_re[i], a_lam_im[i], a_b_re[i], a_b_im[i],
                                              a_c_re[i], a_c_im[i], a_log_dt[i])
            u = _norm_matmul(h, a_norm[i], a_w_in[i].astype(BF16), tm)
            g = _s5_core(u, bbd, cbd, a_re, a_im, a_d[i], n_bhi, seq, s5_steps)
            h = _glu(g, h, a_w_glu[i].astype(BF16), tm)
        else:
            j = layer - n_a
            if not natural:
                h = h.reshape(n_bhi, seq, SUBLANES, d).transpose(0, 2, 1, 3).reshape(m, d)
                natural = True
            scale = HEAD_DIM ** -0.5
            qg = jnp.tile(b_q_norm[j] * scale, LANES // HEAD_DIM).reshape(1, LANES)
            kg = jnp.tile(k_norm, LANES // HEAD_DIM).reshape(1, LANES)
            q, k_new, v_new = _qkv(h, b_norm[j], kv_norm, b_w_q[j].astype(BF16), w_kv.astype(BF16),
                                   qg, kg, tm)
            if j == 0:
                k, v = k_new, v_new
            o = _attention(q.reshape(bsz, seq, d), k.reshape(bsz, seq, d), v.reshape(bsz, seq, d),
                           attn_blk)
            h = _oproj(o.reshape(m, d), h, b_w_o[j].astype(BF16), tm)
        wv, wg, wd = _ffn_weights(ffn_w_up[layer], ffn_w_down[layer], fc)
        shift = 1 if natural else SUBLANES
        rows_per_seq = seq if natural else seq * SUBLANES
        h = _conv_ffn(h, ffn_norm[layer], wv, wg, ffn_conv_w[layer], ffn_conv_b[layer], wd,
                      tm, shift, rows_per_seq)
    if not natural:
        h = h.reshape(n_bhi, seq, SUBLANES, d).transpose(0, 2, 1, 3).reshape(m, d)
    return h.reshape(bsz, seq, d)
```

```python
import functools
import math

import jax
import jax.numpy as jnp
from jax import lax
from jax.experimental import pallas as pl
from jax.experimental.pallas import tpu as pltpu

EPS = 1e-6
HEAD_DIM = 64
SSM_GROUP = 16
SSM_STATE = 64
CONV_W = 3
SUBLANES = 8
LANES = 128
GROUP_BLOCK = 16
ATTN_SUBTILES = 2
VMEM_LIMIT = 56 * 1024 * 1024

BF16 = jnp.bfloat16
F32 = jnp.float32


def _cparams(*sem):
    return pltpu.CompilerParams(dimension_semantics=sem, vmem_limit_bytes=VMEM_LIMIT)


def _resident(shape):
    nd = len(shape)
    return pl.BlockSpec(shape, lambda *_: (0,) * nd, pipeline_mode=pl.Buffered(1))


def _rms(x):
    return lax.rsqrt(jnp.mean(x * x, axis=-1, keepdims=True) + EPS)


def _dot(a, b):
    return jnp.dot(a, b, preferred_element_type=F32)


def _norm_matmul_kernel(x_ref, g_ref, w_ref, o_ref):
    x = x_ref[...]
    xn = (x * _rms(x) * g_ref[...]).astype(BF16)
    o_ref[...] = _dot(xn, w_ref[...])


def _norm_matmul(x, g, w, tm):
    m, d = x.shape
    n = w.shape[1]
    return pl.pallas_call(
        _norm_matmul_kernel,
        grid=(m // tm,),
        in_specs=[pl.BlockSpec((tm, d), lambda i: (i, 0)),
                  _resident((1, d)),
                  _resident((d, n))],
        out_specs=pl.BlockSpec((tm, n), lambda i: (i, 0)),
        out_shape=jax.ShapeDtypeStruct((m, n), F32),
        compiler_params=_cparams("parallel"),
        name="s5_in_proj",
    )(x, g.reshape(1, d), w)


def _s5_kernel(u_ref, bbd_ref, cbd_ref, are_ref, aim_ref, d_ref, o_ref, s_ref, carry_ref,
               *, steps, n_blocks):
    half = GROUP_BLOCK * SSM_STATE
    cb = GROUP_BLOCK * SSM_GROUP
    chunks = 4

    @pl.when(pl.program_id(1) == 0)
    def _():
        carry_ref[...] = jnp.zeros_like(carry_ref)

    for k in range(n_blocks):
        s_ref[...] = _dot(u_ref[:, k * cb:(k + 1) * cb].astype(BF16), bbd_ref[k])

        for c0 in range(0, half // LANES, chunks):
            re_cols = [(c0 + c) * LANES for c in range(chunks)]
            im_cols = [half + col for col in re_cols]
            st_cols = [k * half + col for col in re_cols]
            a_re = [are_ref[:, pl.ds(col, LANES)] for col in st_cols]
            a_im = [aim_ref[:, pl.ds(col, LANES)] for col in st_cols]

            def step(t, state):
                rows = pl.ds(pl.multiple_of(t * SUBLANES, SUBLANES), SUBLANES)
                new = []
                for c in range(chunks):
                    s_re, s_im = state[2 * c], state[2 * c + 1]
                    n_re = a_re[c] * s_re - a_im[c] * s_im + s_ref[rows, pl.ds(re_cols[c], LANES)]
                    n_im = a_re[c] * s_im + a_im[c] * s_re + s_ref[rows, pl.ds(im_cols[c], LANES)]
                    s_ref[rows, pl.ds(re_cols[c], LANES)] = n_re
                    s_ref[rows, pl.ds(im_cols[c], LANES)] = n_im
                    new += [n_re, n_im]
                return tuple(new)

            init = []
            for c in range(chunks):
                init += [carry_ref[0, :, pl.ds(st_cols[c], LANES)],
                         carry_ref[1, :, pl.ds(st_cols[c], LANES)]]
            final = lax.fori_loop(0, steps, step, tuple(init), unroll=2)
            for c in range(chunks):
                carry_ref[0, :, pl.ds(st_cols[c], LANES)] = final[2 * c]
                carry_ref[1, :, pl.ds(st_cols[c], LANES)] = final[2 * c + 1]

        y = _dot(s_ref[...].astype(BF16), cbd_ref[k])
        y = y + d_ref[:, k * cb:(k + 1) * cb] * u_ref[:, k * cb:(k + 1) * cb]
        o_ref[:, k * cb:(k + 1) * cb] = jax.nn.gelu(y).astype(BF16)


def _s5_core(u, bbd, cbd, a_re, a_im, d_skip, n_bhi, seq, steps):
    m, d = u.shape
    n_blocks = bbd.shape[0]
    rows = steps * SUBLANES
    tiles = seq // steps
    n_state = a_re.shape[1]
    kern = functools.partial(_s5_kernel, steps=steps, n_blocks=n_blocks)
    return pl.pallas_call(
        kern,
        grid=(n_bhi, tiles),
        in_specs=[pl.BlockSpec((rows, d), lambda b, t: (b * tiles + t, 0)),
                  _resident(bbd.shape), _resident(cbd.shape),
                  _resident(a_re.shape), _resident(a_im.shape), _resident((1, d))],
        out_specs=pl.BlockSpec((rows, d), lambda b, t: (b * tiles + t, 0)),
        out_shape=jax.ShapeDtypeStruct((m, d), BF16),
        scratch_shapes=[pltpu.VMEM((rows, 2 * GROUP_BLOCK * SSM_STATE), F32),
                        pltpu.VMEM((2, SUBLANES, n_state), F32)],
        compiler_params=_cparams("parallel", "arbitrary"),
        name="s5_core",
    )(u, bbd, cbd, a_re, a_im, d_skip.reshape(1, d))


def _glu_kernel(g_ref, h_ref, w_ref, o_ref):
    d = h_ref.shape[1]
    vg = _dot(g_ref[...], w_ref[...])
    o_ref[...] = h_ref[...] + vg[:, :d] * jax.nn.sigmoid(vg[:, d:])


def _glu(g, h, w, tm):
    m, d = h.shape
    return pl.pallas_call(
        _glu_kernel,
        grid=(m // tm,),
        in_specs=[pl.BlockSpec((tm, d), lambda i: (i, 0)),
                  pl.BlockSpec((tm, d), lambda i: (i, 0)),
                  _resident(w.shape)],
        out_specs=pl.BlockSpec((tm, d), lambda i: (i, 0)),
        out_shape=jax.ShapeDtypeStruct((m, d), F32),
        compiler_params=_cparams("parallel"),
        name="s5_glu",
    )(g, h, w)


def _ffn_kernel(h_ref, g_ref, wv_ref, wg_ref, cw_ref, cb_ref, wd_ref, o_ref,
                xn_ref, gbuf_ref, carry_ref, *, shift, tiles_per_seq):
    tm = h_ref.shape[0]
    n_chunks, _, fc = wv_ref.shape
    hist = (CONV_W - 1) * shift
    pad = gbuf_ref.shape[0] - tm
    first = pl.program_id(0) % tiles_per_seq == 0

    h = h_ref[...]
    xn_ref[...] = (h * _rms(h) * g_ref[...]).astype(BF16)
    acc = h
    for c in range(n_chunks):
        xn = xn_ref[...]
        val = _dot(xn, wv_ref[c])
        gate = _dot(xn, wg_ref[c])

        @pl.when(first)
        def _():
            gbuf_ref[0:pad, :] = jnp.zeros((pad, fc), F32)

        @pl.when(jnp.logical_not(first))
        def _():
            gbuf_ref[0:pad, :] = carry_ref[c]

        gbuf_ref[pad:pad + tm, :] = gate
        carry_ref[c] = gate[tm - pad:tm, :]
        cols = pl.ds(c * fc, fc)
        gc = cb_ref[:, cols] + cw_ref[CONV_W - 1:CONV_W, cols] * gate
        for j in range(CONV_W - 1):
            off = pad - hist + j * shift
            gc = gc + cw_ref[j:j + 1, cols] * gbuf_ref[off:off + tm, :]
        act = (jax.nn.silu(gc) * val).astype(BF16)
        acc = acc + _dot(act, wd_ref[c])
    o_ref[...] = acc


def _conv_ffn(h, g, wv, wg, conv_w, conv_b, wd, tm, shift, rows_per_seq):
    m, d = h.shape
    n_chunks, _, fc = wv.shape
    f = n_chunks * fc
    pad = max(SUBLANES, (CONV_W - 1) * shift)
    kern = functools.partial(_ffn_kernel, shift=shift, tiles_per_seq=rows_per_seq // tm)
    return pl.pallas_call(
        kern,
        grid=(m // tm,),
        in_specs=[pl.BlockSpec((tm, d), lambda i: (i, 0)),
                  _resident((1, d)),
                  _resident(wv.shape), _resident(wg.shape),
                  _resident((CONV_W, f)), _resident((1, f)),
                  _resident(wd.shape)],
        out_specs=pl.BlockSpec((tm, d), lambda i: (i, 0)),
        out_shape=jax.ShapeDtypeStruct((m, d), F32),
        scratch_shapes=[pltpu.VMEM((tm, d), BF16),
                        pltpu.VMEM((pad + tm, fc), F32),
                        pltpu.VMEM((n_chunks, pad, fc), F32)],
        compiler_params=_cparams("arbitrary"),
        name="conv_ffn",
    )(h, g.reshape(1, d), wv, wg, conv_w, conv_b.reshape(1, f), wd)


def _head_norm(x, gain):
    lo = lax.broadcasted_iota(jnp.int32, (1, LANES), 1) < HEAD_DIM
    sq = x * x
    ss_lo = jnp.sum(jnp.where(lo, sq, 0.0), axis=-1, keepdims=True)
    ss_hi = jnp.sum(jnp.where(lo, 0.0, sq), axis=-1, keepdims=True)
    ms = jnp.where(lo, ss_lo, ss_hi) * (1.0 / HEAD_DIM)
    return x * lax.rsqrt(ms + EPS) * gain


def _qkv_kernel(h_ref, gq_ref, gkv_ref, wq_ref, wkv_ref, qg_ref, kg_ref,
                q_ref, k_ref, v_ref):
    d = h_ref.shape[1]
    h = h_ref[...]
    hn = h * _rms(h)
    q = _dot((hn * gq_ref[...]).astype(BF16), wq_ref[...])
    kv = _dot((hn * gkv_ref[...]).astype(BF16), wkv_ref[...])
    for c in range(0, d, LANES):
        q_ref[:, c:c + LANES] = _head_norm(q[:, c:c + LANES], qg_ref[...]).astype(BF16)
        k_ref[:, c:c + LANES] = _head_norm(kv[:, c:c + LANES], kg_ref[...]).astype(BF16)
    v_ref[...] = kv[:, d:].astype(BF16)


def _qkv(h, gq, gkv, wq, wkv, qg, kg, tm):
    m, d = h.shape
    row = pl.BlockSpec((tm, d), lambda i: (i, 0))
    out = jax.ShapeDtypeStruct((m, d), BF16)
    return pl.pallas_call(
        _qkv_kernel,
        grid=(m // tm,),
        in_specs=[row, _resident((1, d)), _resident((1, d)),
                  _resident(wq.shape), _resident(wkv.shape),
                  _resident((1, LANES)), _resident((1, LANES))],
        out_specs=[row, row, row],
        out_shape=[out, out, out],
        compiler_params=_cparams("parallel"),
        name="qkv_proj",
    )(h, gq.reshape(1, d), gkv.reshape(1, d), wq, wkv, qg, kg)


MASKED_LOG = -1e30


def _softplus(z):
    neg_abs = pltpu.bitcast(pltpu.bitcast(z, jnp.uint32) | jnp.uint32(0x80000000), F32)
    return jnp.maximum(z, 0.0) + jnp.log(1.0 + jnp.exp(neg_abs))


def _attn_kernel(q_ref, k_ref, v_ref, o_ref, *, blk):
    i = pl.program_id(2)
    lo = lax.broadcasted_iota(jnp.int32, (1, LANES), 1) < HEAD_DIM
    row = lax.broadcasted_iota(jnp.int32, (blk, blk), 0)
    col = lax.broadcasted_iota(jnp.int32, (blk, blk), 1)
    causal = col < row
    keys_from = jnp.where(row >= col, 1.0, 0.0).astype(BF16)
    keys_from2 = jnp.concatenate([keys_from, keys_from], axis=0)
    qms = []
    for s in range(ATTN_SUBTILES):
        q2 = q_ref[0, s * blk:(s + 1) * blk, :]
        qms.append([jnp.where(lo, q2, jnp.zeros_like(q2)), jnp.where(lo, jnp.zeros_like(q2), q2)])

    def key_rows(j):
        return pl.ds(pl.multiple_of(j * blk, blk), blk)

    def run(specs, state):
        state = list(state)
        chains = [(s, j, masked, hd) for s, j, masked in specs for hd in range(2)]
        zs = [lax.dot_general(qms[s][hd], k_ref[0, key_rows(j), :], (((1,), (1,)), ((), ())),
                              preferred_element_type=F32) for s, j, _, hd in chains]
        sps = [_softplus(z) for z in zs]
        sps = [jnp.where(causal, sp, 0.0) if ch[2] else sp for sp, ch in zip(sps, chains)]
        zs = [jnp.where(causal, z, MASKED_LOG) if ch[2] else z for z, ch in zip(zs, chains)]
        splits = []
        for sp in sps:
            hi = sp.astype(BF16)
            splits.append(jnp.concatenate([hi, (sp - hi.astype(F32)).astype(BF16)], axis=1))
        cums = [_dot(split, keys_from2) for split in splits]
        for c, (s, j, _, hd) in enumerate(chains):
            acc, carry = 2 * (2 * s + hd), 2 * (2 * s + hd) + 1
            w = jnp.exp(zs[c] - cums[c] - state[carry]).astype(BF16)
            state[acc] = state[acc] + _dot(w, v_ref[0, key_rows(j), :])
            state[carry] = state[carry] + cums[c][:, 0:1]
        return tuple(state)

    state = (jnp.zeros((blk, LANES), F32), jnp.zeros((blk, 1), F32)) * (2 * ATTN_SUBTILES)
    first = i * ATTN_SUBTILES
    diagonal = []
    for j in reversed(range(ATTN_SUBTILES)):
        diagonal += [(s, first + j, s == j) for s in range(j, ATTN_SUBTILES)]
    state = run(diagonal, state)
    state = lax.fori_loop(
        0, first,
        lambda jj, st: run([(s, first - 1 - jj, False) for s in range(ATTN_SUBTILES)], st), state)
    for s in range(ATTN_SUBTILES):
        o_ref[0, s * blk:(s + 1) * blk, :] = jnp.where(
            lo, state[2 * (2 * s)], state[2 * (2 * s + 1)]).astype(BF16)


def _attention(q, k, v, blk):
    b, seq, d = q.shape
    tq = blk * ATTN_SUBTILES
    qspec = pl.BlockSpec((1, tq, LANES), lambda bi, hp, i: (bi, i, hp))
    kvspec = pl.BlockSpec((1, seq, LANES), lambda bi, hp, i: (bi, 0, hp))
    return pl.pallas_call(
        functools.partial(_attn_kernel, blk=blk),
        grid=(b, d // LANES, seq // tq),
        in_specs=[qspec, kvspec, kvspec],
        out_specs=qspec,
        out_shape=jax.ShapeDtypeStruct((b, seq, d), BF16),
        compiler_params=_cparams("parallel", "parallel", "arbitrary"),
        name="sb_attention",
    )(q, k, v)


def _oproj_kernel(o_ref, h_ref, w_ref, out_ref):
    out_ref[...] = h_ref[...] + _dot(o_ref[...], w_ref[...])


def _oproj(o, h, w, tm):
    m, d = h.shape
    row = pl.BlockSpec((tm, d), lambda i: (i, 0))
    return pl.pallas_call(
        _oproj_kernel,
        grid=(m // tm,),
        in_specs=[row, row, _resident(w.shape)],
        out_specs=row,
        out_shape=jax.ShapeDtypeStruct((m, d), F32),
        compiler_params=_cparams("parallel"),
        name="attn_out_proj",
    )(o, h, w)


def _s5_tables(lam_re, lam_im, b_re, b_im, c_re, c_im, log_dt):
    n_groups, n_state = lam_re.shape
    n_ch = b_re.shape[2]
    n_blocks = n_groups // GROUP_BLOCK
    dt = jnp.exp(log_dt)[:, None]
    mag = jnp.exp(lam_re * dt)
    a_bar_re, a_bar_im = mag * jnp.cos(lam_im * dt), mag * jnp.sin(lam_im * dt)
    den = lam_re * lam_re + lam_im * lam_im
    k_re = (((a_bar_re - 1.0) * lam_re + a_bar_im * lam_im) / den)[..., None]
    k_im = ((a_bar_im * lam_re - (a_bar_re - 1.0) * lam_im) / den)[..., None]
    b_bar_re = k_re * b_re - k_im * b_im
    b_bar_im = k_re * b_im + k_im * b_re
    eye = jnp.eye(GROUP_BLOCK, dtype=F32)
    bb = jnp.stack([b_bar_re, b_bar_im]).reshape(2, n_blocks, GROUP_BLOCK, n_state, n_ch)
    bbd = jnp.einsum('rkgph,gj->kghrjp', bb, eye).reshape(
        n_blocks, GROUP_BLOCK * n_ch, 2 * GROUP_BLOCK * n_state)
    cc = jnp.stack([c_re, -c_im]).reshape(2, n_blocks, GROUP_BLOCK, n_ch, n_state)
    cbd = jnp.einsum('rkghp,gj->krgpjh', cc, eye).reshape(
        n_blocks, 2 * GROUP_BLOCK * n_state, GROUP_BLOCK * n_ch)
    a_re = jnp.broadcast_to(a_bar_re.reshape(1, -1), (SUBLANES, n_groups * n_state))
    a_im = jnp.broadcast_to(a_bar_im.reshape(1, -1), (SUBLANES, n_groups * n_state))
    return bbd.astype(BF16), cbd.astype(BF16), a_re, a_im


def _ffn_weights(w_up, w_down, fc):
    d, f2 = w_up.shape
    f = f2 // 2
    n = f // fc
    wv = w_up[:, :f].reshape(d, n, fc).transpose(1, 0, 2).astype(BF16)
    wg = w_up[:, f:].reshape(d, n, fc).transpose(1, 0, 2).astype(BF16)
    wd = w_down.reshape(n, fc, d).astype(BF16)
    return wv, wg, wd


def kernel(x, a_norm, a_w_in, a_lam_re, a_lam_im, a_b_re, a_b_im, a_c_re, a_c_im, a_d, a_log_dt, a_w_glu, kv_norm, w_kv, k_norm, b_norm, b_w_q, b_q_norm, b_w_o, ffn_norm, ffn_w_up, ffn_conv_w, ffn_conv_b, ffn_w_down):
    bsz, seq, d = x.shape
    n_a = a_norm.shape[0]
    depth = ffn_norm.shape[0]
    m = bsz * seq
    n_bhi = bsz // SUBLANES
    tm = 512
    fc = 256
    s5_steps = min(128, seq)
    attn_blk = min(256, seq)
    assert bsz % SUBLANES == 0 and seq % s5_steps == 0 and seq % (attn_blk * ATTN_SUBTILES) == 0
    assert (seq * SUBLANES) % tm == 0 and seq % tm == 0 and d % LANES == 0

    h = x.reshape(n_bhi, SUBLANES, seq, d).transpose(0, 2, 1, 3).reshape(m, d)
    k = v = None
    natural = False
    for layer in range(depth):
        if layer < n_a:
            i = layer
            bbd, cbd, a_re, a_im = _s5_tables(a_lam_re[i], a_lam_im[i], a_b_re[i], a_b_im[i],
                                              a_c_re[i], a_c_im[i], a_log_dt[i])
            u = _norm_matmul(h, a_norm[i], a_w_in[i].astype(BF16), tm)
            g = _s5_core(u, bbd, cbd, a_re, a_im, a_d[i], n_bhi, seq, s5_steps)
            h = _glu(g, h, a_w_glu[i].astype(BF16), tm)
        else:
            j = layer - n_a
            if not natural:
                h = h.reshape(n_bhi, seq, SUBLANES, d).transpose(0, 2, 1, 3).reshape(m, d)
                natural = True
            scale = HEAD_DIM ** -0.5
            qg = jnp.tile(b_q_norm[j] * scale, LANES // HEAD_DIM).reshape(1, LANES)
            kg = jnp.tile(k_norm, LANES // HEAD_DIM).reshape(1, LANES)
            q, k_new, v_new = _qkv(h, b_norm[j], kv_norm, b_w_q[j].astype(BF16), w_kv.astype(BF16),
                                   qg, kg, tm)
            if j == 0:
                k, v = k_new, v_new
            o = _attention(q.reshape(bsz, seq, d), k.reshape(bsz, seq, d), v.reshape(bsz, seq, d),
                           attn_blk)
            h = _oproj(o.reshape(m, d), h, b_w_o[j].astype(BF16), tm)
        wv, wg, wd = _ffn_weights(ffn_w_up[layer], ffn_w_down[layer], fc)
        shift = 1 if natural else SUBLANES
        rows_per_seq = seq if natural else seq * SUBLANES
        h = _conv_ffn(h, ffn_norm[layer], wv, wg, ffn_conv_w[layer], ffn_conv_b[layer], wd,
                      tm, shift, rows_per_seq)
    if not natural:
        h = h.reshape(n_bhi, seq, SUBLANES, d).transpose(0, 2, 1, 3).reshape(m, d)
    return h.reshape(bsz, seq, d)
```

```python
import functools
import math

import jax
import jax.numpy as jnp
from jax import lax
from jax.experimental import pallas as pl
from jax.experimental.pallas import tpu as pltpu

EPS = 1e-6
HEAD_DIM = 64
SSM_GROUP = 16
SSM_STATE = 64
CONV_W = 3
SUBLANES = 8
LANES = 128
GROUP_BLOCK = 16
ATTN_GROUP = 2
VMEM_LIMIT = 56 * 1024 * 1024

BF16 = jnp.bfloat16
F32 = jnp.float32


def _cparams(*sem):
    return pltpu.CompilerParams(dimension_semantics=sem, vmem_limit_bytes=VMEM_LIMIT)


def _resident(shape):
    nd = len(shape)
    return pl.BlockSpec(shape, lambda *_: (0,) * nd, pipeline_mode=pl.Buffered(1))


def _rms(x):
    return lax.rsqrt(jnp.mean(x * x, axis=-1, keepdims=True) + EPS)


def _dot(a, b):
    return jnp.dot(a, b, preferred_element_type=F32)


def _norm_matmul_kernel(x_ref, g_ref, w_ref, o_ref):
    x = x_ref[...]
    xn = (x * _rms(x) * g_ref[...]).astype(BF16)
    o_ref[...] = _dot(xn, w_ref[...])


def _norm_matmul(x, g, w, tm):
    m, d = x.shape
    n = w.shape[1]
    return pl.pallas_call(
        _norm_matmul_kernel,
        grid=(m // tm,),
        in_specs=[pl.BlockSpec((tm, d), lambda i: (i, 0)),
                  _resident((1, d)),
                  _resident((d, n))],
        out_specs=pl.BlockSpec((tm, n), lambda i: (i, 0)),
        out_shape=jax.ShapeDtypeStruct((m, n), F32),
        compiler_params=_cparams("parallel"),
        name="s5_in_proj",
    )(x, g.reshape(1, d), w)


def _s5_kernel(u_ref, bbd_ref, cbd_ref, are_ref, aim_ref, d_ref, o_ref, s_ref, carry_ref,
               *, steps, n_blocks):
    half = GROUP_BLOCK * SSM_STATE
    cb = GROUP_BLOCK * SSM_GROUP
    chunks = 4

    @pl.when(pl.program_id(1) == 0)
    def _():
        carry_ref[...] = jnp.zeros_like(carry_ref)

    for k in range(n_blocks):
        s_ref[...] = _dot(u_ref[:, k * cb:(k + 1) * cb].astype(BF16), bbd_ref[k])

        for c0 in range(0, half // LANES, chunks):
            re_cols = [(c0 + c) * LANES for c in range(chunks)]
            im_cols = [half + col for col in re_cols]
            st_cols = [k * half + col for col in re_cols]
            a_re = [are_ref[:, pl.ds(col, LANES)] for col in st_cols]
            a_im = [aim_ref[:, pl.ds(col, LANES)] for col in st_cols]

            def step(t, state):
                rows = pl.ds(pl.multiple_of(t * SUBLANES, SUBLANES), SUBLANES)
                new = []
                for c in range(chunks):
                    s_re, s_im = state[2 * c], state[2 * c + 1]
                    n_re = a_re[c] * s_re - a_im[c] * s_im + s_ref[rows, pl.ds(re_cols[c], LANES)]
                    n_im = a_re[c] * s_im + a_im[c] * s_re + s_ref[rows, pl.ds(im_cols[c], LANES)]
                    s_ref[rows, pl.ds(re_cols[c], LANES)] = n_re
                    s_ref[rows, pl.ds(im_cols[c], LANES)] = n_im
                    new += [n_re, n_im]
                return tuple(new)

            init = []
            for c in range(chunks):
                init += [carry_ref[0, :, pl.ds(st_cols[c], LANES)],
                         carry_ref[1, :, pl.ds(st_cols[c], LANES)]]
            final = lax.fori_loop(0, steps, step, tuple(init), unroll=2)
            for c in range(chunks):
                carry_ref[0, :, pl.ds(st_cols[c], LANES)] = final[2 * c]
                carry_ref[1, :, pl.ds(st_cols[c], LANES)] = final[2 * c + 1]

        y = _dot(s_ref[...].astype(BF16), cbd_ref[k])
        y = y + d_ref[:, k * cb:(k + 1) * cb] * u_ref[:, k * cb:(k + 1) * cb]
        o_ref[:, k * cb:(k + 1) * cb] = jax.nn.gelu(y).astype(BF16)


def _s5_core(u, bbd, cbd, a_re, a_im, d_skip, n_bhi, seq, steps):
    m, d = u.shape
    n_blocks = bbd.shape[0]
    rows = steps * SUBLANES
    tiles = seq // steps
    n_state = a_re.shape[1]
    kern = functools.partial(_s5_kernel, steps=steps, n_blocks=n_blocks)
    return pl.pallas_call(
        kern,
        grid=(n_bhi, tiles),
        in_specs=[pl.BlockSpec((rows, d), lambda b, t: (b * tiles + t, 0)),
                  _resident(bbd.shape), _resident(cbd.shape),
                  _resident(a_re.shape), _resident(a_im.shape), _resident((1, d))],
        out_specs=pl.BlockSpec((rows, d), lambda b, t: (b * tiles + t, 0)),
        out_shape=jax.ShapeDtypeStruct((m, d), BF16),
        scratch_shapes=[pltpu.VMEM((rows, 2 * GROUP_BLOCK * SSM_STATE), F32),
                        pltpu.VMEM((2, SUBLANES, n_state), F32)],
        compiler_params=_cparams("parallel", "arbitrary"),
        name="s5_core",
    )(u, bbd, cbd, a_re, a_im, d_skip.reshape(1, d))


def _glu_kernel(g_ref, h_ref, w_ref, o_ref):
    d = h_ref.shape[1]
    vg = _dot(g_ref[...], w_ref[...])
    o_ref[...] = h_ref[...] + vg[:, :d] * jax.nn.sigmoid(vg[:, d:])


def _glu(g, h, w, tm):
    m, d = h.shape
    return pl.pallas_call(
        _glu_kernel,
        grid=(m // tm,),
        in_specs=[pl.BlockSpec((tm, d), lambda i: (i, 0)),
                  pl.BlockSpec((tm, d), lambda i: (i, 0)),
                  _resident(w.shape)],
        out_specs=pl.BlockSpec((tm, d), lambda i: (i, 0)),
        out_shape=jax.ShapeDtypeStruct((m, d), F32),
        compiler_params=_cparams("parallel"),
        name="s5_glu",
    )(g, h, w)


def _ffn_kernel(h_ref, g_ref, wv_ref, wg_ref, cw_ref, cb_ref, wd_ref, o_ref,
                xn_ref, gbuf_ref, carry_ref, *, shift, tiles_per_seq):
    tm = h_ref.shape[0]
    n_chunks, _, fc = wv_ref.shape
    hist = (CONV_W - 1) * shift
    pad = gbuf_ref.shape[0] - tm

    @pl.when(pl.program_id(0) % tiles_per_seq == 0)
    def _():
        carry_ref[...] = jnp.zeros_like(carry_ref)

    h = h_ref[...]
    xn_ref[...] = (h * _rms(h) * g_ref[...]).astype(BF16)
    acc = h
    for c in range(n_chunks):
        xn = xn_ref[...]
        val = _dot(xn, wv_ref[c])
        gate = _dot(xn, wg_ref[c])
        cols = pl.ds(c * fc, fc)
        gc = cb_ref[:, cols] + cw_ref[CONV_W - 1:CONV_W, cols] * gate
        gbuf_ref[0:pad, :] = carry_ref[c]
        gbuf_ref[pad:pad + tm, :] = gate
        for j in range(CONV_W - 1):
            off = pad - hist + j * shift
            gc = gc + cw_ref[j:j + 1, cols] * gbuf_ref[off:off + tm, :]
        carry_ref[c] = gate[tm - pad:tm, :]
        act = (jax.nn.silu(gc) * val).astype(BF16)
        acc = acc + _dot(act, wd_ref[c])
    o_ref[...] = acc


def _conv_ffn(h, g, wv, wg, conv_w, conv_b, wd, tm, shift, rows_per_seq):
    m, d = h.shape
    n_chunks, _, fc = wv.shape
    f = n_chunks * fc
    pad = max(SUBLANES, (CONV_W - 1) * shift)
    kern = functools.partial(_ffn_kernel, shift=shift, tiles_per_seq=rows_per_seq // tm)
    return pl.pallas_call(
        kern,
        grid=(m // tm,),
        in_specs=[pl.BlockSpec((tm, d), lambda i: (i, 0)),
                  _resident((1, d)),
                  _resident(wv.shape), _resident(wg.shape),
                  _resident((CONV_W, f)), _resident((1, f)),
                  _resident(wd.shape)],
        out_specs=pl.BlockSpec((tm, d), lambda i: (i, 0)),
        out_shape=jax.ShapeDtypeStruct((m, d), F32),
        scratch_shapes=[pltpu.VMEM((tm, d), BF16),
                        pltpu.VMEM((pad + tm, fc), F32),
                        pltpu.VMEM((n_chunks, pad, fc), F32)],
        compiler_params=_cparams("arbitrary"),
        name="conv_ffn",
    )(h, g.reshape(1, d), wv, wg, conv_w, conv_b.reshape(1, f), wd)


def _head_norm(x, gain):
    lo = lax.broadcasted_iota(jnp.int32, (1, LANES), 1) < HEAD_DIM
    sq = x * x
    ss_lo = jnp.sum(jnp.where(lo, sq, 0.0), axis=-1, keepdims=True)
    ss_hi = jnp.sum(jnp.where(lo, 0.0, sq), axis=-1, keepdims=True)
    ms = jnp.where(lo, ss_lo, ss_hi) * (1.0 / HEAD_DIM)
    return x * lax.rsqrt(ms + EPS) * gain


def _qkv_kernel(h_ref, gq_ref, gkv_ref, wq_ref, wkv_ref, qg_ref, kg_ref,
                q_ref, k_ref, v_ref):
    d = h_ref.shape[1]
    h = h_ref[...]
    hn = h * _rms(h)
    q = _dot((hn * gq_ref[...]).astype(BF16), wq_ref[...])
    kv = _dot((hn * gkv_ref[...]).astype(BF16), wkv_ref[...])
    for c in range(0, d, LANES):
        q_ref[:, c:c + LANES] = _head_norm(q[:, c:c + LANES], qg_ref[...]).astype(BF16)
        k_ref[:, c:c + LANES] = _head_norm(kv[:, c:c + LANES], kg_ref[...]).astype(BF16)
    v_ref[...] = kv[:, d:].astype(BF16)


def _qkv(h, gq, gkv, wq, wkv, qg, kg, tm):
    m, d = h.shape
    row = pl.BlockSpec((tm, d), lambda i: (i, 0))
    out = jax.ShapeDtypeStruct((m, d), BF16)
    return pl.pallas_call(
        _qkv_kernel,
        grid=(m // tm,),
        in_specs=[row, _resident((1, d)), _resident((1, d)),
                  _resident(wq.shape), _resident(wkv.shape),
                  _resident((1, LANES)), _resident((1, LANES))],
        out_specs=[row, row, row],
        out_shape=[out, out, out],
        compiler_params=_cparams("parallel"),
        name="qkv_proj",
    )(h, gq.reshape(1, d), gkv.reshape(1, d), wq, wkv, qg, kg)


MASKED_LOG = -1e30


def _softplus(z):
    neg_abs = pltpu.bitcast(pltpu.bitcast(z, jnp.uint32) | jnp.uint32(0x80000000), F32)
    return jnp.maximum(z, 0.0) + jnp.log(1.0 + jnp.exp(neg_abs))


def _attn_kernel(q_ref, k_ref, v_ref, o_ref, *, blk):
    n_sub = q_ref.shape[1] // blk
    lo = lax.broadcasted_iota(jnp.int32, (1, LANES), 1) < HEAD_DIM
    row = lax.broadcasted_iota(jnp.int32, (blk, blk), 0)
    col = lax.broadcasted_iota(jnp.int32, (blk, blk), 1)
    causal = col < row
    keys_from = jnp.where(row >= col, 1.0, 0.0).astype(BF16)

    def rows(j):
        return slice(j * blk, (j + 1) * blk)

    def q_rows(s, hd):
        q2 = q_ref[0, rows(s), :]
        return jnp.where(lo, q2, jnp.zeros_like(q2)) if hd == 0 else jnp.where(lo, jnp.zeros_like(q2), q2)

    def run(j, subtiles, accs, carries):
        chains = [(s, hd) for s in subtiles for hd in range(2)]
        kb, vb = k_ref[0, rows(j), :], v_ref[0, rows(j), :]
        zs = [lax.dot_general(q_rows(s, hd), kb, (((1,), (1,)), ((), ())),
                              preferred_element_type=F32) for s, hd in chains]
        sps = [_softplus(z) for z in zs]
        sps = [jnp.where(causal, sp, 0.0) if s == j else sp for sp, (s, _) in zip(sps, chains)]
        zs = [jnp.where(causal, z, MASKED_LOG) if s == j else z for z, (s, _) in zip(zs, chains)]
        cums = [_dot(sp.astype(BF16), keys_from) for sp in sps]
        for c, ch in enumerate(chains):
            w = jnp.exp(zs[c] - cums[c] - carries[ch]).astype(BF16)
            accs[ch] = accs[ch] + _dot(w, vb)
            carries[ch] = carries[ch] + cums[c][:, 0:1]

    accs = {(s, hd): jnp.zeros((blk, LANES), F32) for s in range(n_sub) for hd in range(2)}
    carries = {(s, hd): jnp.zeros((blk, 1), F32) for s in range(n_sub) for hd in range(2)}
    for j in reversed(range(n_sub)):
        subtiles = list(range(j, n_sub))
        for g in range(0, len(subtiles), ATTN_GROUP):
            run(j, subtiles[g:g + ATTN_GROUP], accs, carries)
    for s in range(n_sub):
        o_ref[0, rows(s), :] = jnp.where(lo, accs[(s, 0)], accs[(s, 1)]).astype(BF16)


def _attention(q, k, v, blk):
    b, seq, d = q.shape
    spec = pl.BlockSpec((1, seq, LANES), lambda bi, hp: (bi, 0, hp))
    return pl.pallas_call(
        functools.partial(_attn_kernel, blk=blk),
        grid=(b, d // LANES),
        in_specs=[spec, spec, spec],
        out_specs=spec,
        out_shape=jax.ShapeDtypeStruct((b, seq, d), BF16),
        compiler_params=_cparams("parallel", "parallel"),
        name="sb_attention",
    )(q, k, v)


def _oproj_kernel(o_ref, h_ref, w_ref, out_ref):
    out_ref[...] = h_ref[...] + _dot(o_ref[...], w_ref[...])


def _oproj(o, h, w, tm):
    m, d = h.shape
    row = pl.BlockSpec((tm, d), lambda i: (i, 0))
    return pl.pallas_call(
        _oproj_kernel,
        grid=(m // tm,),
        in_specs=[row, row, _resident(w.shape)],
        out_specs=row,
        out_shape=jax.ShapeDtypeStruct((m, d), F32),
        compiler_params=_cparams("parallel"),
        name="attn_out_proj",
    )(o, h, w)


def _s5_tables(lam_re, lam_im, b_re, b_im, c_re, c_im, log_dt):
    n_groups, n_state = lam_re.shape
    n_ch = b_re.shape[2]
    n_blocks = n_groups // GROUP_BLOCK
    dt = jnp.exp(log_dt)[:, None]
    mag = jnp.exp(lam_re * dt)
    a_bar_re, a_bar_im = mag * jnp.cos(lam_im * dt), mag * jnp.sin(lam_im * dt)
    den = lam_re * lam_re + lam_im * lam_im
    k_re = (((a_bar_re - 1.0) * lam_re + a_bar_im * lam_im) / den)[..., None]
    k_im = ((a_bar_im * lam_re - (a_bar_re - 1.0) * lam_im) / den)[..., None]
    b_bar_re = k_re * b_re - k_im * b_im
    b_bar_im = k_re * b_im + k_im * b_re
    eye = jnp.eye(GROUP_BLOCK, dtype=F32)
    bb = jnp.stack([b_bar_re, b_bar_im]).reshape(2, n_blocks, GROUP_BLOCK, n_state, n_ch)
    bbd = jnp.einsum('rkgph,gj->kghrjp', bb, eye).reshape(
        n_blocks, GROUP_BLOCK * n_ch, 2 * GROUP_BLOCK * n_state)
    cc = jnp.stack([c_re, -c_im]).reshape(2, n_blocks, GROUP_BLOCK, n_ch, n_state)
    cbd = jnp.einsum('rkghp,gj->krgpjh', cc, eye).reshape(
        n_blocks, 2 * GROUP_BLOCK * n_state, GROUP_BLOCK * n_ch)
    a_re = jnp.broadcast_to(a_bar_re.reshape(1, -1), (SUBLANES, n_groups * n_state))
    a_im = jnp.broadcast_to(a_bar_im.reshape(1, -1), (SUBLANES, n_groups * n_state))
    return bbd.astype(BF16), cbd.astype(BF16), a_re, a_im


def _ffn_weights(w_up, w_down, fc):
    d, f2 = w_up.shape
    f = f2 // 2
    n = f // fc
    wv = w_up[:, :f].reshape(d, n, fc).transpose(1, 0, 2).astype(BF16)
    wg = w_up[:, f:].reshape(d, n, fc).transpose(1, 0, 2).astype(BF16)
    wd = w_down.reshape(n, fc, d).astype(BF16)
    return wv, wg, wd


def kernel(x, a_norm, a_w_in, a_lam_re, a_lam_im, a_b_re, a_b_im, a_c_re, a_c_im, a_d, a_log_dt, a_w_glu, kv_norm, w_kv, k_norm, b_norm, b_w_q, b_q_norm, b_w_o, ffn_norm, ffn_w_up, ffn_conv_w, ffn_conv_b, ffn_w_down):
    bsz, seq, d = x.shape
    n_a = a_norm.shape[0]
    depth = ffn_norm.shape[0]
    m = bsz * seq
    n_bhi = bsz // SUBLANES
    tm = 512
    fc = 256
    s5_steps = min(128, seq)
    attn_blk = min(256, seq)
    assert bsz % SUBLANES == 0 and seq % s5_steps == 0 and seq % attn_blk == 0
    assert (seq * SUBLANES) % tm == 0 and seq % tm == 0 and d % LANES == 0

    h = x.reshape(n_bhi, SUBLANES, seq, d).transpose(0, 2, 1, 3).reshape(m, d)
    k = v = None
    natural = False
    for layer in range(depth):
        if layer < n_a:
            i = layer
            bbd, cbd, a_re, a_im = _s5_tables(a_lam_re[i], a_lam_im[i], a_b_re[i], a_b_im[i],
                                              a_c_re[i], a_c_im[i], a_log_dt[i])
            u = _norm_matmul(h, a_norm[i], a_w_in[i].astype(BF16), tm)
            g = _s5_core(u, bbd, cbd, a_re, a_im, a_d[i], n_bhi, seq, s5_steps)
            h = _glu(g, h, a_w_glu[i].astype(BF16), tm)
        else:
            j = layer - n_a
            if not natural:
                h = h.reshape(n_bhi, seq, SUBLANES, d).transpose(0, 2, 1, 3).reshape(m, d)
                natural = True
            scale = HEAD_DIM ** -0.5
            qg = jnp.tile(b_q_norm[j] * scale, LANES // HEAD_DIM).reshape(1, LANES)
            kg = jnp.tile(k_norm, LANES // HEAD_DIM).reshape(1, LANES)
            q, k_new, v_new = _qkv(h, b_norm[j], kv_norm, b_w_q[j].astype(BF16), w_kv.astype(BF16),
                                   qg, kg, tm)
            if j == 0:
                k, v = k_new, v_new
            o = _attention(q.reshape(bsz, seq, d), k.reshape(bsz, seq, d), v.reshape(bsz, seq, d),
                           attn_blk)
            h = _oproj(o.reshape(m, d), h, b_w_o[j].astype(BF16), tm)
        wv, wg, wd = _ffn_weights(ffn_w_up[layer], ffn_w_down[layer], fc)
        shift = 1 if natural else SUBLANES
        rows_per_seq = seq if natural else seq * SUBLANES
        h = _conv_ffn(h, ffn_norm[layer], wv, wg, ffn_conv_w[layer], ffn_conv_b[layer], wd,
                      tm, shift, rows_per_seq)
    if not natural:
        h = h.reshape(n_bhi, seq, SUBLANES, d).transpose(0, 2, 1, 3).reshape(m, d)
    return h.reshape(bsz, seq, d)
```

```python
import functools
import math

import jax
import jax.numpy as jnp
from jax import lax
from jax.experimental import pallas as pl
from jax.experimental.pallas import tpu as pltpu

EPS = 1e-6
HEAD_DIM = 64
SSM_GROUP = 16
SSM_STATE = 64
CONV_W = 3
SUBLANES = 8
LANES = 128
GROUP_BLOCK = 16
ATTN_GROUP = 2
ATTN_BLOCK = 256
ROW_TILE = 512
S5_STEPS = 128
FFN_CHUNK = 256
VMEM_LIMIT = 56 * 1024 * 1024

BF16 = jnp.bfloat16
F32 = jnp.float32


def _cparams(*sem):
    return pltpu.CompilerParams(dimension_semantics=sem, vmem_limit_bytes=VMEM_LIMIT)


def _resident(shape):
    nd = len(shape)
    return pl.BlockSpec(shape, lambda *_: (0,) * nd, pipeline_mode=pl.Buffered(1))


def _rms(x):
    return lax.rsqrt(jnp.mean(x * x, axis=-1, keepdims=True) + EPS)


def _dot(a, b):
    return jnp.dot(a, b, preferred_element_type=F32)


def _interleaved_rows(x_ref):
    return jnp.concatenate([x_ref[:, t, :] for t in range(x_ref.shape[1])], axis=0)


def _s5_kernel(x_ref, g_ref, win_ref, bbd_ref, cbd_ref, are_ref, aim_ref, d_ref, o_ref,
               u_ref, s_ref, carry_ref, *, steps, n_blocks):
    half = GROUP_BLOCK * SSM_STATE
    cb = GROUP_BLOCK * SSM_GROUP
    chunks = 4

    @pl.when(pl.program_id(1) == 0)
    def _():
        carry_ref[...] = jnp.zeros_like(carry_ref)

    x = _interleaved_rows(x_ref)
    u_ref[...] = _dot((x * _rms(x) * g_ref[...]).astype(BF16), win_ref[...])

    for k in range(n_blocks):
        s_ref[...] = _dot(u_ref[:, k * cb:(k + 1) * cb].astype(BF16), bbd_ref[k])

        for c0 in range(0, half // LANES, chunks):
            re_cols = [(c0 + c) * LANES for c in range(chunks)]
            im_cols = [half + col for col in re_cols]
            st_cols = [k * half + col for col in re_cols]
            a_re = [are_ref[:, pl.ds(col, LANES)] for col in st_cols]
            a_im = [aim_ref[:, pl.ds(col, LANES)] for col in st_cols]

            def step(t, state):
                rows = pl.ds(pl.multiple_of(t * SUBLANES, SUBLANES), SUBLANES)
                new = []
                for c in range(chunks):
                    s_re, s_im = state[2 * c], state[2 * c + 1]
                    n_re = a_re[c] * s_re - a_im[c] * s_im + s_ref[rows, pl.ds(re_cols[c], LANES)]
                    n_im = a_re[c] * s_im + a_im[c] * s_re + s_ref[rows, pl.ds(im_cols[c], LANES)]
                    s_ref[rows, pl.ds(re_cols[c], LANES)] = n_re
                    s_ref[rows, pl.ds(im_cols[c], LANES)] = n_im
                    new += [n_re, n_im]
                return tuple(new)

            init = []
            for c in range(chunks):
                init += [carry_ref[0, :, pl.ds(st_cols[c], LANES)],
                         carry_ref[1, :, pl.ds(st_cols[c], LANES)]]
            final = lax.fori_loop(0, steps, step, tuple(init), unroll=2)
            for c in range(chunks):
                carry_ref[0, :, pl.ds(st_cols[c], LANES)] = final[2 * c]
                carry_ref[1, :, pl.ds(st_cols[c], LANES)] = final[2 * c + 1]

        y = _dot(s_ref[...].astype(BF16), cbd_ref[k])
        y = y + d_ref[:, k * cb:(k + 1) * cb] * u_ref[:, k * cb:(k + 1) * cb]
        o_ref[:, k * cb:(k + 1) * cb] = jax.nn.gelu(y).astype(BF16)


def _s5_core(x, g, w_in, bbd, cbd, a_re, a_im, d_skip, steps):
    bsz, seq, d = x.shape
    n_blocks = bbd.shape[0]
    rows = steps * SUBLANES
    tiles = seq // steps
    n_state = a_re.shape[1]
    kern = functools.partial(_s5_kernel, steps=steps, n_blocks=n_blocks)
    return pl.pallas_call(
        kern,
        grid=(bsz // SUBLANES, tiles),
        in_specs=[pl.BlockSpec((SUBLANES, steps, d), lambda b, t: (b, t, 0)),
                  _resident((1, d)), _resident(w_in.shape),
                  _resident(bbd.shape), _resident(cbd.shape),
                  _resident(a_re.shape), _resident(a_im.shape), _resident((1, d))],
        out_specs=pl.BlockSpec((rows, d), lambda b, t: (b * tiles + t, 0)),
        out_shape=jax.ShapeDtypeStruct((bsz * seq, d), BF16),
        scratch_shapes=[pltpu.VMEM((rows, d), F32),
                        pltpu.VMEM((rows, 2 * GROUP_BLOCK * SSM_STATE), F32),
                        pltpu.VMEM((2, SUBLANES, n_state), F32)],
        compiler_params=_cparams("parallel", "arbitrary"),
        name="s5_core",
    )(x, g.reshape(1, d), w_in, bbd, cbd, a_re, a_im, d_skip.reshape(1, d))


def _ffn_kernel(a_ref, resid_ref, wmix_ref, g_ref, wup_ref, cw_ref, cb_ref, wd_ref, o_ref,
                xn_ref, act_ref, gbuf_ref, carry_ref, *, shift, tiles_per_seq, glu, interleaved):
    tm, d = a_ref.shape
    n_chunks, _, fc = carry_ref.shape
    f = wd_ref.shape[0]
    hist = (CONV_W - 1) * shift
    pad = gbuf_ref.shape[0] - tm

    @pl.when(pl.program_id(0) % tiles_per_seq == 0)
    def _():
        carry_ref[...] = jnp.zeros_like(carry_ref)

    mix = _dot(a_ref[...], wmix_ref[...])
    if glu:
        mix = mix[:, :d] * jax.nn.sigmoid(mix[:, d:])
    h = (_interleaved_rows(resid_ref) if interleaved else resid_ref[...]) + mix
    xn_ref[...] = (h * _rms(h) * g_ref[...]).astype(BF16)
    acc = h
    for c in range(n_chunks):
        xn = xn_ref[...]
        val = _dot(xn, wup_ref[:, c * fc:(c + 1) * fc])
        gate = _dot(xn, wup_ref[:, f + c * fc:f + (c + 1) * fc])
        cols = pl.ds(c * fc, fc)
        gc = cb_ref[:, cols] + cw_ref[CONV_W - 1:CONV_W, cols] * gate
        gbuf_ref[0:pad, :] = carry_ref[c]
        gbuf_ref[pad:pad + tm, :] = gate
        for j in range(CONV_W - 1):
            off = pad - hist + j * shift
            gc = gc + cw_ref[j:j + 1, cols] * gbuf_ref[off:off + tm, :]
        carry_ref[c] = gate[tm - pad:tm, :]
        act_ref[:, c * fc:(c + 1) * fc] = (jax.nn.silu(gc) * val).astype(BF16)
    acc = acc + _dot(act_ref[...], wd_ref[...])
    if interleaved:
        for t in range(tm // SUBLANES):
            o_ref[:, t, :] = acc[t * SUBLANES:(t + 1) * SUBLANES, :]
    else:
        o_ref[...] = acc


def _mix_ffn(a, resid, w_mix, g, w_up, conv_w, conv_b, w_down, tm, fc, glu, interleaved):
    bsz, seq, d = resid.shape
    m = bsz * seq
    f = w_down.shape[0]
    assert f % fc == 0
    n_chunks = f // fc
    shift = SUBLANES if interleaved else 1
    rows_per_seq = seq * shift
    tiles_per_seq = rows_per_seq // tm
    pad = max(SUBLANES, (CONV_W - 1) * shift)
    if interleaved:
        nat = pl.BlockSpec((SUBLANES, tm // SUBLANES, d),
                           lambda i: (i // tiles_per_seq, i % tiles_per_seq, 0))
        out_shape = jax.ShapeDtypeStruct((bsz, seq, d), F32)
    else:
        resid = resid.reshape(m, d)
        nat = pl.BlockSpec((tm, d), lambda i: (i, 0))
        out_shape = jax.ShapeDtypeStruct((m, d), F32)
    kern = functools.partial(_ffn_kernel, shift=shift, tiles_per_seq=tiles_per_seq, glu=glu,
                             interleaved=interleaved)
    out = pl.pallas_call(
        kern,
        grid=(m // tm,),
        in_specs=[pl.BlockSpec((tm, d), lambda i: (i, 0)), nat,
                  _resident(w_mix.shape), _resident((1, d)),
                  _resident(w_up.shape),
                  _resident((CONV_W, f)), _resident((1, f)),
                  _resident(w_down.shape)],
        out_specs=nat,
        out_shape=out_shape,
        scratch_shapes=[pltpu.VMEM((tm, d), BF16),
                        pltpu.VMEM((tm, f), BF16),
                        pltpu.VMEM((pad + tm, fc), F32),
                        pltpu.VMEM((n_chunks, pad, fc), F32)],
        compiler_params=_cparams("arbitrary"),
        name="mix_ffn",
    )(a, resid, w_mix, g.reshape(1, d), w_up, conv_w, conv_b.reshape(1, f), w_down)
    return out.reshape(bsz, seq, d)


def _head_norm(x, gain):
    lo = lax.broadcasted_iota(jnp.int32, (1, LANES), 1) < HEAD_DIM
    sq = x * x
    ss_lo = jnp.sum(jnp.where(lo, sq, 0.0), axis=-1, keepdims=True)
    ss_hi = jnp.sum(jnp.where(lo, 0.0, sq), axis=-1, keepdims=True)
    ms = jnp.where(lo, ss_lo, ss_hi) * (1.0 / HEAD_DIM)
    return x * lax.rsqrt(ms + EPS) * gain


def _qkv_kernel(h_ref, gq_ref, gkv_ref, wq_ref, wkv_ref, qg_ref, kg_ref,
                q_ref, k_ref, v_ref):
    d = h_ref.shape[1]
    h = h_ref[...]
    hn = h * _rms(h)
    q = _dot((hn * gq_ref[...]).astype(BF16), wq_ref[...])
    kv = _dot((hn * gkv_ref[...]).astype(BF16), wkv_ref[...])
    for c in range(0, d, LANES):
        q_ref[:, c:c + LANES] = _head_norm(q[:, c:c + LANES], qg_ref[...]).astype(BF16)
        k_ref[:, c:c + LANES] = _head_norm(kv[:, c:c + LANES], kg_ref[...]).astype(BF16)
    v_ref[...] = kv[:, d:].astype(BF16)


def _qkv(h, gq, gkv, wq, wkv, qg, kg, tm):
    m, d = h.shape
    row = pl.BlockSpec((tm, d), lambda i: (i, 0))
    out = jax.ShapeDtypeStruct((m, d), BF16)
    return pl.pallas_call(
        _qkv_kernel,
        grid=(m // tm,),
        in_specs=[row, _resident((1, d)), _resident((1, d)),
                  _resident(wq.shape), _resident(wkv.shape),
                  _resident((1, LANES)), _resident((1, LANES))],
        out_specs=[row, row, row],
        out_shape=[out, out, out],
        compiler_params=_cparams("parallel"),
        name="qkv_proj",
    )(h, gq.reshape(1, d), gkv.reshape(1, d), wq, wkv, qg, kg)


MASKED_LOG = -1e30


def _softplus(z):
    neg_abs = pltpu.bitcast(pltpu.bitcast(z, jnp.uint32) | jnp.uint32(0x80000000), F32)
    return jnp.maximum(z, 0.0) + jnp.log(1.0 + jnp.exp(neg_abs))


def _attn_kernel(q_ref, k_ref, v_ref, o_ref, *, blk):
    n_sub = q_ref.shape[1] // blk
    lo = lax.broadcasted_iota(jnp.int32, (1, LANES), 1) < HEAD_DIM
    row = lax.broadcasted_iota(jnp.int32, (blk, blk), 0)
    col = lax.broadcasted_iota(jnp.int32, (blk, blk), 1)
    causal = col < row
    keys_from = jnp.where(row >= col, 1.0, 0.0).astype(BF16)

    def rows(j):
        return slice(j * blk, (j + 1) * blk)

    def q_rows(s, hd):
        q2 = q_ref[0, rows(s), :]
        return jnp.where(lo, q2, jnp.zeros_like(q2)) if hd == 0 else jnp.where(lo, jnp.zeros_like(q2), q2)

    def run(j, subtiles, accs, carries):
        chains = [(s, hd) for s in subtiles for hd in range(2)]
        kb, vb = k_ref[0, rows(j), :], v_ref[0, rows(j), :]
        zs = [lax.dot_general(q_rows(s, hd), kb, (((1,), (1,)), ((), ())),
                              preferred_element_type=F32) for s, hd in chains]
        sps = [_softplus(z) for z in zs]
        sps = [jnp.where(causal, sp, 0.0) if s == j else sp for sp, (s, _) in zip(sps, chains)]
        zs = [jnp.where(causal, z, MASKED_LOG) if s == j else z for z, (s, _) in zip(zs, chains)]
        cums = [_dot(sp.astype(BF16), keys_from) for sp in sps]
        for c, ch in enumerate(chains):
            w = jnp.exp(zs[c] - cums[c] - carries[ch]).astype(BF16)
            accs[ch] = accs[ch] + _dot(w, vb)
            carries[ch] = carries[ch] + cums[c][:, 0:1]

    accs = {(s, hd): jnp.zeros((blk, LANES), F32) for s in range(n_sub) for hd in range(2)}
    carries = {(s, hd): jnp.zeros((blk, 1), F32) for s in range(n_sub) for hd in range(2)}
    for j in reversed(range(n_sub)):
        subtiles = list(range(j, n_sub))
        for g in range(0, len(subtiles), ATTN_GROUP):
            run(j, subtiles[g:g + ATTN_GROUP], accs, carries)
    for s in range(n_sub):
        o_ref[0, rows(s), :] = jnp.where(lo, accs[(s, 0)], accs[(s, 1)]).astype(BF16)


def _attention(q, k, v, blk):
    b, seq, d = q.shape
    spec = pl.BlockSpec((1, seq, LANES), lambda bi, hp: (bi, 0, hp))
    return pl.pallas_call(
        functools.partial(_attn_kernel, blk=blk),
        grid=(b, d // LANES),
        in_specs=[spec, spec, spec],
        out_specs=spec,
        out_shape=jax.ShapeDtypeStruct((b, seq, d), BF16),
        compiler_params=_cparams("parallel", "parallel"),
        name="sb_attention",
    )(q, k, v)


def _s5_tables(lam_re, lam_im, b_re, b_im, c_re, c_im, log_dt):
    n_groups, n_state = lam_re.shape
    n_ch = b_re.shape[2]
    n_blocks = n_groups // GROUP_BLOCK
    dt = jnp.exp(log_dt)[:, None]
    mag = jnp.exp(lam_re * dt)
    a_bar_re, a_bar_im = mag * jnp.cos(lam_im * dt), mag * jnp.sin(lam_im * dt)
    den = lam_re * lam_re + lam_im * lam_im
    k_re = (((a_bar_re - 1.0) * lam_re + a_bar_im * lam_im) / den)[..., None]
    k_im = ((a_bar_im * lam_re - (a_bar_re - 1.0) * lam_im) / den)[..., None]
    b_bar_re = k_re * b_re - k_im * b_im
    b_bar_im = k_re * b_im + k_im * b_re
    eye = jnp.eye(GROUP_BLOCK, dtype=F32)
    bb = jnp.stack([b_bar_re, b_bar_im]).reshape(2, n_blocks, GROUP_BLOCK, n_state, n_ch)
    bbd = jnp.einsum('rkgph,gj->kghrjp', bb, eye).reshape(
        n_blocks, GROUP_BLOCK * n_ch, 2 * GROUP_BLOCK * n_state)
    cc = jnp.stack([c_re, -c_im]).reshape(2, n_blocks, GROUP_BLOCK, n_ch, n_state)
    cbd = jnp.einsum('rkghp,gj->krgpjh', cc, eye).reshape(
        n_blocks, 2 * GROUP_BLOCK * n_state, GROUP_BLOCK * n_ch)
    a_re = jnp.broadcast_to(a_bar_re.reshape(1, -1), (SUBLANES, n_groups * n_state))
    a_im = jnp.broadcast_to(a_bar_im.reshape(1, -1), (SUBLANES, n_groups * n_state))
    return bbd.astype(BF16), cbd.astype(BF16), a_re, a_im


def kernel(x, a_norm, a_w_in, a_lam_re, a_lam_im, a_b_re, a_b_im, a_c_re, a_c_im, a_d, a_log_dt, a_w_glu, kv_norm, w_kv, k_norm, b_norm, b_w_q, b_q_norm, b_w_o, ffn_norm, ffn_w_up, ffn_conv_w, ffn_conv_b, ffn_w_down):
    bsz, seq, d = x.shape
    n_a = a_norm.shape[0]
    depth = ffn_norm.shape[0]
    m = bsz * seq
    tm = min(ROW_TILE, seq)
    s5_steps = min(S5_STEPS, seq)
    attn_blk = min(ATTN_BLOCK, seq)
    assert bsz % SUBLANES == 0 and seq % s5_steps == 0 and seq % attn_blk == 0
    assert seq % tm == 0 and d % LANES == 0

    h = x
    k = v = None
    for layer in range(depth):
        if layer < n_a:
            i = layer
            bbd, cbd, a_re, a_im = _s5_tables(a_lam_re[i], a_lam_im[i], a_b_re[i], a_b_im[i],
                                              a_c_re[i], a_c_im[i], a_log_dt[i])
            mixed = _s5_core(h, a_norm[i], a_w_in[i].astype(BF16), bbd, cbd, a_re, a_im, a_d[i],
                             s5_steps)
            w_mix, glu = a_w_glu[i].astype(BF16), True
        else:
            j = layer - n_a
            scale = HEAD_DIM ** -0.5
            qg = jnp.tile(b_q_norm[j] * scale, LANES // HEAD_DIM).reshape(1, LANES)
            kg = jnp.tile(k_norm, LANES // HEAD_DIM).reshape(1, LANES)
            q, k_new, v_new = _qkv(h.reshape(m, d), b_norm[j], kv_norm, b_w_q[j].astype(BF16),
                                   w_kv.astype(BF16), qg, kg, tm)
            if j == 0:
                k, v = k_new.reshape(bsz, seq, d), v_new.reshape(bsz, seq, d)
            mixed = _attention(q.reshape(bsz, seq, d), k, v, attn_blk).reshape(m, d)
            w_mix, glu = b_w_o[j].astype(BF16), False
        h = _mix_ffn(mixed, h, w_mix, ffn_norm[layer], ffn_w_up[layer].astype(BF16),
                     ffn_conv_w[layer], ffn_conv_b[layer], ffn_w_down[layer].astype(BF16),
                     tm, FFN_CHUNK, glu, interleaved=layer < n_a)
    return h
```

```python
import functools
import math

import jax
import jax.numpy as jnp
from jax import lax
from jax.experimental import pallas as pl
from jax.experimental.pallas import tpu as pltpu

EPS = 1e-6
HEAD_DIM = 64
SSM_GROUP = 16
SSM_STATE = 64
CONV_W = 3
SUBLANES = 8
LANES = 128
GROUP_BLOCK = 16
ATTN_GROUP = 2
ATTN_BLOCK = 256
ROW_TILE = 512
S5_STEPS = 128
FFN_CHUNK = 256
VMEM_LIMIT = 56 * 1024 * 1024

BF16 = jnp.bfloat16
F32 = jnp.float32


def _cparams(*sem):
    return pltpu.CompilerParams(dimension_semantics=sem, vmem_limit_bytes=VMEM_LIMIT)


def _resident(shape):
    nd = len(shape)
    return pl.BlockSpec(shape, lambda *_: (0,) * nd, pipeline_mode=pl.Buffered(1))


def _rms(x):
    return lax.rsqrt(jnp.mean(x * x, axis=-1, keepdims=True) + EPS)


def _dot(a, b):
    return jnp.dot(a, b, preferred_element_type=F32)


def _interleaved_rows(x_ref):
    return jnp.concatenate([x_ref[:, t, :] for t in range(x_ref.shape[1])], axis=0)


def _s5_kernel(x_ref, g_ref, win_ref, bbd_ref, cbd_ref, are_ref, aim_ref, d_ref, o_ref,
               u_ref, s_ref, carry_ref, *, steps, n_blocks):
    half = GROUP_BLOCK * SSM_STATE
    cb = GROUP_BLOCK * SSM_GROUP
    chunks = 4

    @pl.when(pl.program_id(1) == 0)
    def _():
        carry_ref[...] = jnp.zeros_like(carry_ref)

    x = _interleaved_rows(x_ref)
    u_ref[...] = _dot((x * _rms(x) * g_ref[...]).astype(BF16), win_ref[...])

    def input_map(k):
        s_ref[k % 2] = _dot(u_ref[:, k * cb:(k + 1) * cb].astype(BF16), bbd_ref[k])

    input_map(0)
    for k in range(n_blocks):
        if k + 1 < n_blocks:
            input_map(k + 1)
        s_k = s_ref.at[k % 2]
        for c0 in range(0, half // LANES, chunks):
            re_cols = [(c0 + c) * LANES for c in range(chunks)]
            im_cols = [half + col for col in re_cols]
            st_cols = [k * half + col for col in re_cols]
            a_re = [are_ref[:, col:col + LANES] for col in st_cols]
            a_im = [aim_ref[:, col:col + LANES] for col in st_cols]
            s_re = [carry_ref[0, :, col:col + LANES] for col in st_cols]
            s_im = [carry_ref[1, :, col:col + LANES] for col in st_cols]
            for t in range(steps):
                rows = slice(t * SUBLANES, (t + 1) * SUBLANES)
                for c in range(chunks):
                    re, im = slice(re_cols[c], re_cols[c] + LANES), slice(im_cols[c], im_cols[c] + LANES)
                    n_re = a_re[c] * s_re[c] - a_im[c] * s_im[c] + s_k[rows, re]
                    n_im = a_re[c] * s_im[c] + a_im[c] * s_re[c] + s_k[rows, im]
                    s_k[rows, re] = n_re
                    s_k[rows, im] = n_im
                    s_re[c], s_im[c] = n_re, n_im
            for c in range(chunks):
                carry_ref[0, :, st_cols[c]:st_cols[c] + LANES] = s_re[c]
                carry_ref[1, :, st_cols[c]:st_cols[c] + LANES] = s_im[c]

        y = _dot(s_k[...].astype(BF16), cbd_ref[k])
        y = y + d_ref[:, k * cb:(k + 1) * cb] * u_ref[:, k * cb:(k + 1) * cb]
        o_ref[:, k * cb:(k + 1) * cb] = jax.nn.gelu(y).astype(BF16)


def _s5_core(x, g, w_in, bbd, cbd, a_re, a_im, d_skip, steps):
    bsz, seq, d = x.shape
    n_blocks = bbd.shape[0]
    rows = steps * SUBLANES
    tiles = seq // steps
    n_state = a_re.shape[1]
    kern = functools.partial(_s5_kernel, steps=steps, n_blocks=n_blocks)
    return pl.pallas_call(
        kern,
        grid=(bsz // SUBLANES, tiles),
        in_specs=[pl.BlockSpec((SUBLANES, steps, d), lambda b, t: (b, t, 0)),
                  _resident((1, d)), _resident(w_in.shape),
                  _resident(bbd.shape), _resident(cbd.shape),
                  _resident(a_re.shape), _resident(a_im.shape), _resident((1, d))],
        out_specs=pl.BlockSpec((rows, d), lambda b, t: (b * tiles + t, 0)),
        out_shape=jax.ShapeDtypeStruct((bsz * seq, d), BF16),
        scratch_shapes=[pltpu.VMEM((rows, d), F32),
                        pltpu.VMEM((2, rows, 2 * GROUP_BLOCK * SSM_STATE), F32),
                        pltpu.VMEM((2, SUBLANES, n_state), F32)],
        compiler_params=_cparams("parallel", "arbitrary"),
        name="s5_core",
    )(x, g.reshape(1, d), w_in, bbd, cbd, a_re, a_im, d_skip.reshape(1, d))


def _ffn_kernel(a_ref, resid_ref, wmix_ref, g_ref, wup_ref, cw_ref, cb_ref, wd_ref, o_ref,
                xn_ref, act_ref, gbuf_ref, carry_ref, *, shift, tiles_per_seq, glu, interleaved):
    tm, d = a_ref.shape
    n_chunks, _, fc = carry_ref.shape
    f = wd_ref.shape[0]
    hist = (CONV_W - 1) * shift
    pad = gbuf_ref.shape[0] - tm

    @pl.when(pl.program_id(0) % tiles_per_seq == 0)
    def _():
        carry_ref[...] = jnp.zeros_like(carry_ref)

    mix = _dot(a_ref[...], wmix_ref[...])
    if glu:
        mix = mix[:, :d] * jax.nn.sigmoid(mix[:, d:])
    h = (_interleaved_rows(resid_ref) if interleaved else resid_ref[...]) + mix
    xn_ref[...] = (h * _rms(h) * g_ref[...]).astype(BF16)
    acc = h
    for c in range(n_chunks):
        xn = xn_ref[...]
        val = _dot(xn, wup_ref[:, c * fc:(c + 1) * fc])
        gate = _dot(xn, wup_ref[:, f + c * fc:f + (c + 1) * fc])
        cols = pl.ds(c * fc, fc)
        gc = cb_ref[:, cols] + cw_ref[CONV_W - 1:CONV_W, cols] * gate
        gbuf_ref[0:pad, :] = carry_ref[c]
        gbuf_ref[pad:pad + tm, :] = gate
        for j in range(CONV_W - 1):
            off = pad - hist + j * shift
            gc = gc + cw_ref[j:j + 1, cols] * gbuf_ref[off:off + tm, :]
        carry_ref[c] = gate[tm - pad:tm, :]
        act_ref[:, c * fc:(c + 1) * fc] = (jax.nn.silu(gc) * val).astype(BF16)
    acc = acc + _dot(act_ref[...], wd_ref[...])
    if interleaved:
        for t in range(tm // SUBLANES):
            o_ref[:, t, :] = acc[t * SUBLANES:(t + 1) * SUBLANES, :]
    else:
        o_ref[...] = acc


def _mix_ffn(a, resid, w_mix, g, w_up, conv_w, conv_b, w_down, tm, fc, glu, interleaved):
    bsz, seq, d = resid.shape
    m = bsz * seq
    f = w_down.shape[0]
    assert f % fc == 0
    n_chunks = f // fc
    shift = SUBLANES if interleaved else 1
    rows_per_seq = seq * shift
    tiles_per_seq = rows_per_seq // tm
    pad = max(SUBLANES, (CONV_W - 1) * shift)
    if interleaved:
        nat = pl.BlockSpec((SUBLANES, tm // SUBLANES, d),
                           lambda i: (i // tiles_per_seq, i % tiles_per_seq, 0))
        out_shape = jax.ShapeDtypeStruct((bsz, seq, d), F32)
    else:
        resid = resid.reshape(m, d)
        nat = pl.BlockSpec((tm, d), lambda i: (i, 0))
        out_shape = jax.ShapeDtypeStruct((m, d), F32)
    kern = functools.partial(_ffn_kernel, shift=shift, tiles_per_seq=tiles_per_seq, glu=glu,
                             interleaved=interleaved)
    out = pl.pallas_call(
        kern,
        grid=(m // tm,),
        in_specs=[pl.BlockSpec((tm, d), lambda i: (i, 0)), nat,
                  _resident(w_mix.shape), _resident((1, d)),
                  _resident(w_up.shape),
                  _resident((CONV_W, f)), _resident((1, f)),
                  _resident(w_down.shape)],
        out_specs=nat,
        out_shape=out_shape,
        scratch_shapes=[pltpu.VMEM((tm, d), BF16),
                        pltpu.VMEM((tm, f), BF16),
                        pltpu.VMEM((pad + tm, fc), F32),
                        pltpu.VMEM((n_chunks, pad, fc), F32)],
        compiler_params=_cparams("arbitrary"),
        name="mix_ffn",
    )(a, resid, w_mix, g.reshape(1, d), w_up, conv_w, conv_b.reshape(1, f), w_down)
    return out.reshape(bsz, seq, d)


def _head_norm(x, gain):
    lo = lax.broadcasted_iota(jnp.int32, (1, LANES), 1) < HEAD_DIM
    sq = x * x
    ss_lo = jnp.sum(jnp.where(lo, sq, 0.0), axis=-1, keepdims=True)
    ss_hi = jnp.sum(jnp.where(lo, 0.0, sq), axis=-1, keepdims=True)
    ms = jnp.where(lo, ss_lo, ss_hi) * (1.0 / HEAD_DIM)
    return x * lax.rsqrt(ms + EPS) * gain


def _qkv_kernel(h_ref, gq_ref, gkv_ref, wq_ref, wkv_ref, qg_ref, kg_ref,
                q_ref, k_ref, v_ref):
    d = h_ref.shape[1]
    h = h_ref[...]
    hn = h * _rms(h)
    q = _dot((hn * gq_ref[...]).astype(BF16), wq_ref[...])
    kv = _dot((hn * gkv_ref[...]).astype(BF16), wkv_ref[...])
    for c in range(0, d, LANES):
        q_ref[:, c:c + LANES] = _head_norm(q[:, c:c + LANES], qg_ref[...]).astype(BF16)
        k_ref[:, c:c + LANES] = _head_norm(kv[:, c:c + LANES], kg_ref[...]).astype(BF16)
    v_ref[...] = kv[:, d:].astype(BF16)


def _qkv(h, gq, gkv, wq, wkv, qg, kg, tm):
    m, d = h.shape
    row = pl.BlockSpec((tm, d), lambda i: (i, 0))
    out = jax.ShapeDtypeStruct((m, d), BF16)
    return pl.pallas_call(
        _qkv_kernel,
        grid=(m // tm,),
        in_specs=[row, _resident((1, d)), _resident((1, d)),
                  _resident(wq.shape), _resident(wkv.shape),
                  _resident((1, LANES)), _resident((1, LANES))],
        out_specs=[row, row, row],
        out_shape=[out, out, out],
        compiler_params=_cparams("parallel"),
        name="qkv_proj",
    )(h, gq.reshape(1, d), gkv.reshape(1, d), wq, wkv, qg, kg)


MASKED_LOG = -1e30


SOFTPLUS_LINEAR = 40.0


def _softplus(z):
    return jnp.maximum(z, jnp.log(1.0 + jnp.exp(jnp.minimum(z, SOFTPLUS_LINEAR))))


def _attn_kernel(q_ref, k_ref, v_ref, o_ref, *, blk):
    n_sub = q_ref.shape[1] // blk
    lo = lax.broadcasted_iota(jnp.int32, (1, LANES), 1) < HEAD_DIM
    row = lax.broadcasted_iota(jnp.int32, (blk, blk), 0)
    col = lax.broadcasted_iota(jnp.int32, (blk, blk), 1)
    causal = col < row
    keys_from = jnp.where(row >= col, 1.0, 0.0).astype(BF16)

    def rows(j):
        return slice(j * blk, (j + 1) * blk)

    def q_rows(s, hd):
        q2 = q_ref[0, rows(s), :]
        return jnp.where(lo, q2, jnp.zeros_like(q2)) if hd == 0 else jnp.where(lo, jnp.zeros_like(q2), q2)

    def front(j, subtiles):
        chains = [(s, hd) for s in subtiles for hd in range(2)]
        kb = k_ref[0, rows(j), :]
        zs = [lax.dot_general(q_rows(s, hd), kb, (((1,), (1,)), ((), ())),
                              preferred_element_type=F32) for s, hd in chains]
        sps = [_softplus(z) for z in zs]
        sps = [jnp.where(causal, sp, 0.0) if s == j else sp for sp, (s, _) in zip(sps, chains)]
        zs = [jnp.where(causal, z, MASKED_LOG) if s == j else z for z, (s, _) in zip(zs, chains)]
        return chains, zs, [sp.astype(BF16) for sp in sps]

    def back(j, chains, zs, sps, accs, carries):
        vb = v_ref[0, rows(j), :]
        cums = [_dot(sp, keys_from) for sp in sps]
        for c, ch in enumerate(chains):
            w = jnp.exp((zs[c] - cums[c] - carries[ch]).astype(BF16))
            accs[ch] = accs[ch] + _dot(w, vb)
            carries[ch] = carries[ch] + cums[c][:, 0:1]

    accs = {(s, hd): jnp.zeros((blk, LANES), F32) for s in range(n_sub) for hd in range(2)}
    carries = {(s, hd): jnp.zeros((blk, 1), F32) for s in range(n_sub) for hd in range(2)}
    groups = []
    for j in reversed(range(n_sub)):
        subtiles = list(range(j, n_sub))
        groups += [(j, subtiles[g:g + ATTN_GROUP]) for g in range(0, len(subtiles), ATTN_GROUP)]
    pending = None
    for j, subtiles in groups:
        fronted = front(j, subtiles)
        if pending is not None:
            back(*pending, accs, carries)
        pending = (j, *fronted)
    back(*pending, accs, carries)
    for s in range(n_sub):
        o_ref[0, rows(s), :] = jnp.where(lo, accs[(s, 0)], accs[(s, 1)]).astype(BF16)


def _attention(q, k, v, blk):
    b, seq, d = q.shape
    spec = pl.BlockSpec((1, seq, LANES), lambda bi, hp: (bi, 0, hp))
    return pl.pallas_call(
        functools.partial(_attn_kernel, blk=blk),
        grid=(b, d // LANES),
        in_specs=[spec, spec, spec],
        out_specs=spec,
        out_shape=jax.ShapeDtypeStruct((b, seq, d), BF16),
        compiler_params=_cparams("parallel", "parallel"),
        name="sb_attention",
    )(q, k, v)


def _s5_tables(lam_re, lam_im, b_re, b_im, c_re, c_im, log_dt):
    n_groups, n_state = lam_re.shape
    n_ch = b_re.shape[2]
    n_blocks = n_groups // GROUP_BLOCK
    dt = jnp.exp(log_dt)[:, None]
    mag = jnp.exp(lam_re * dt)
    a_bar_re, a_bar_im = mag * jnp.cos(lam_im * dt), mag * jnp.sin(lam_im * dt)
    den = lam_re * lam_re + lam_im * lam_im
    k_re = (((a_bar_re - 1.0) * lam_re + a_bar_im * lam_im) / den)[..., None]
    k_im = ((a_bar_im * lam_re - (a_bar_re - 1.0) * lam_im) / den)[..., None]
    b_bar_re = k_re * b_re - k_im * b_im
    b_bar_im = k_re * b_im + k_im * b_re
    eye = jnp.eye(GROUP_BLOCK, dtype=F32)
    bb = jnp.stack([b_bar_re, b_bar_im]).reshape(2, n_blocks, GROUP_BLOCK, n_state, n_ch)
    bbd = jnp.einsum('rkgph,gj->kghrjp', bb, eye).reshape(
        n_blocks, GROUP_BLOCK * n_ch, 2 * GROUP_BLOCK * n_state)
    cc = jnp.stack([c_re, -c_im]).reshape(2, n_blocks, GROUP_BLOCK, n_ch, n_state)
    cbd = jnp.einsum('rkghp,gj->krgpjh', cc, eye).reshape(
        n_blocks, 2 * GROUP_BLOCK * n_state, GROUP_BLOCK * n_ch)
    a_re = jnp.broadcast_to(a_bar_re.reshape(1, -1), (SUBLANES, n_groups * n_state))
    a_im = jnp.broadcast_to(a_bar_im.reshape(1, -1), (SUBLANES, n_groups * n_state))
    return bbd.astype(BF16), cbd.astype(BF16), a_re, a_im


def kernel(x, a_norm, a_w_in, a_lam_re, a_lam_im, a_b_re, a_b_im, a_c_re, a_c_im, a_d, a_log_dt, a_w_glu, kv_norm, w_kv, k_norm, b_norm, b_w_q, b_q_norm, b_w_o, ffn_norm, ffn_w_up, ffn_conv_w, ffn_conv_b, ffn_w_down):
    bsz, seq, d = x.shape
    n_a = a_norm.shape[0]
    depth = ffn_norm.shape[0]
    m = bsz * seq
    tm = min(ROW_TILE, seq)
    s5_steps = min(S5_STEPS, seq)
    attn_blk = min(ATTN_BLOCK, seq)
    assert bsz % SUBLANES == 0 and seq % s5_steps == 0 and seq % attn_blk == 0
    assert seq % tm == 0 and d % LANES == 0

    h = x
    k = v = None
    for layer in range(depth):
        if layer < n_a:
            i = layer
            bbd, cbd, a_re, a_im = _s5_tables(a_lam_re[i], a_lam_im[i], a_b_re[i], a_b_im[i],
                                              a_c_re[i], a_c_im[i], a_log_dt[i])
            mixed = _s5_core(h, a_norm[i], a_w_in[i].astype(BF16), bbd, cbd, a_re, a_im, a_d[i],
                             s5_steps)
            w_mix, glu = a_w_glu[i].astype(BF16), True
        else:
            j = layer - n_a
            scale = HEAD_DIM ** -0.5
            qg = jnp.tile(b_q_norm[j] * scale, LANES // HEAD_DIM).reshape(1, LANES)
            kg = jnp.tile(k_norm, LANES // HEAD_DIM).reshape(1, LANES)
            q, k_new, v_new = _qkv(h.reshape(m, d), b_norm[j], kv_norm, b_w_q[j].astype(BF16),
                                   w_kv.astype(BF16), qg, kg, tm)
            if j == 0:
                k, v = k_new.reshape(bsz, seq, d), v_new.reshape(bsz, seq, d)
            mixed = _attention(q.reshape(bsz, seq, d), k, v, attn_blk).reshape(m, d)
            w_mix, glu = b_w_o[j].astype(BF16), False
        h = _mix_ffn(mixed, h, w_mix, ffn_norm[layer], ffn_w_up[layer].astype(BF16),
                     ffn_conv_w[layer], ffn_conv_b[layer], ffn_w_down[layer].astype(BF16),
                     tm, FFN_CHUNK, glu, interleaved=layer < n_a)
    return h
```

```python
import functools
import math

import jax
import jax.numpy as jnp
from jax import lax
from jax.experimental import pallas as pl
from jax.experimental.pallas import tpu as pltpu

EPS = 1e-6
HEAD_DIM = 64
SSM_GROUP = 16
SSM_STATE = 64
CONV_W = 3
SUBLANES = 8
LANES = 128
GROUP_BLOCK = 16
ATTN_GROUP = 2
ATTN_BLOCK = 256
ROW_TILE = 512
S5_STEPS = 128
FFN_CHUNK = 256
VMEM_LIMIT = 56 * 1024 * 1024

BF16 = jnp.bfloat16
F32 = jnp.float32


def _cparams(*sem):
    return pltpu.CompilerParams(dimension_semantics=sem, vmem_limit_bytes=VMEM_LIMIT)


def _resident(shape):
    nd = len(shape)
    return pl.BlockSpec(shape, lambda *_: (0,) * nd, pipeline_mode=pl.Buffered(1))


def _rms(x):
    return lax.rsqrt(jnp.mean(x * x, axis=-1, keepdims=True) + EPS)


def _dot(a, b):
    return jnp.dot(a, b, preferred_element_type=F32)


def _interleaved_rows(x_ref):
    return jnp.concatenate([x_ref[:, t, :] for t in range(x_ref.shape[1])], axis=0)


def _s5_kernel(x_ref, g_ref, win_ref, bbd_ref, cbd_ref, are_ref, aim_ref, d_ref, o_ref,
               u_ref, s_ref, carry_ref, *, steps, n_blocks):
    half = GROUP_BLOCK * SSM_STATE
    cb = GROUP_BLOCK * SSM_GROUP
    chunks = 4

    @pl.when(pl.program_id(1) == 0)
    def _():
        carry_ref[...] = jnp.zeros_like(carry_ref)

    x = _interleaved_rows(x_ref)
    u_ref[...] = _dot((x * _rms(x) * g_ref[...]).astype(BF16), win_ref[...])

    def input_map(k):
        s_ref[k % 2] = _dot(u_ref[:, k * cb:(k + 1) * cb].astype(BF16), bbd_ref[k])

    def output_map(k):
        y = _dot(s_ref[k % 2].astype(BF16), cbd_ref[k])
        y = y + d_ref[:, k * cb:(k + 1) * cb] * u_ref[:, k * cb:(k + 1) * cb]
        o_ref[:, k * cb:(k + 1) * cb] = jax.nn.gelu(y).astype(BF16)

    def recurrence(k):
        s_k = s_ref.at[k % 2]
        for c0 in range(0, half // LANES, chunks):
            re_cols = [(c0 + c) * LANES for c in range(chunks)]
            im_cols = [half + col for col in re_cols]
            st_cols = [k * half + col for col in re_cols]
            a_re = [are_ref[:, col:col + LANES] for col in st_cols]
            a_im = [aim_ref[:, col:col + LANES] for col in st_cols]
            s_re = [carry_ref[0, :, col:col + LANES] for col in st_cols]
            s_im = [carry_ref[1, :, col:col + LANES] for col in st_cols]
            for t in range(steps):
                rows = slice(t * SUBLANES, (t + 1) * SUBLANES)
                for c in range(chunks):
                    re, im = slice(re_cols[c], re_cols[c] + LANES), slice(im_cols[c], im_cols[c] + LANES)
                    n_re = a_re[c] * s_re[c] - a_im[c] * s_im[c] + s_k[rows, re]
                    n_im = a_re[c] * s_im[c] + a_im[c] * s_re[c] + s_k[rows, im]
                    s_k[rows, re] = n_re
                    s_k[rows, im] = n_im
                    s_re[c], s_im[c] = n_re, n_im
            for c in range(chunks):
                carry_ref[0, :, st_cols[c]:st_cols[c] + LANES] = s_re[c]
                carry_ref[1, :, st_cols[c]:st_cols[c] + LANES] = s_im[c]

    input_map(0)
    if n_blocks > 1:
        input_map(1)
    for k in range(n_blocks):
        if k >= 1:
            output_map(k - 1)
            if k + 1 < n_blocks:
                input_map(k + 1)
        recurrence(k)
    output_map(n_blocks - 1)


def _s5_core(x, g, w_in, bbd, cbd, a_re, a_im, d_skip, steps):
    bsz, seq, d = x.shape
    n_blocks = bbd.shape[0]
    rows = steps * SUBLANES
    tiles = seq // steps
    n_state = a_re.shape[1]
    kern = functools.partial(_s5_kernel, steps=steps, n_blocks=n_blocks)
    return pl.pallas_call(
        kern,
        grid=(bsz // SUBLANES, tiles),
        in_specs=[pl.BlockSpec((SUBLANES, steps, d), lambda b, t: (b, t, 0)),
                  _resident((1, d)), _resident(w_in.shape),
                  _resident(bbd.shape), _resident(cbd.shape),
                  _resident(a_re.shape), _resident(a_im.shape), _resident((1, d))],
        out_specs=pl.BlockSpec((rows, d), lambda b, t: (b * tiles + t, 0)),
        out_shape=jax.ShapeDtypeStruct((bsz * seq, d), BF16),
        scratch_shapes=[pltpu.VMEM((rows, d), F32),
                        pltpu.VMEM((2, rows, 2 * GROUP_BLOCK * SSM_STATE), F32),
                        pltpu.VMEM((2, SUBLANES, n_state), F32)],
        compiler_params=_cparams("parallel", "arbitrary"),
        name="s5_core",
    )(x, g.reshape(1, d), w_in, bbd, cbd, a_re, a_im, d_skip.reshape(1, d))


def _ffn_kernel(a_ref, resid_ref, wmix_ref, g_ref, wup_ref, cw_ref, cb_ref, wd_ref, o_ref,
                xn_ref, act_ref, gbuf_ref, carry_ref, *, shift, tiles_per_seq, glu, interleaved):
    tm, d = a_ref.shape
    n_chunks, _, fc = carry_ref.shape
    f = wd_ref.shape[0]
    hist = (CONV_W - 1) * shift
    pad = gbuf_ref.shape[0] - tm

    @pl.when(pl.program_id(0) % tiles_per_seq == 0)
    def _():
        carry_ref[...] = jnp.zeros_like(carry_ref)

    mix = _dot(a_ref[...], wmix_ref[...])
    if glu:
        mix = mix[:, :d] * jax.nn.sigmoid(mix[:, d:])
    h = (_interleaved_rows(resid_ref) if interleaved else resid_ref[...]) + mix
    xn_ref[...] = (h * _rms(h) * g_ref[...]).astype(BF16)
    acc = h
    for c in range(n_chunks):
        xn = xn_ref[...]
        val = _dot(xn, wup_ref[:, c * fc:(c + 1) * fc])
        gate = _dot(xn, wup_ref[:, f + c * fc:f + (c + 1) * fc])
        cols = pl.ds(c * fc, fc)
        gc = cb_ref[:, cols] + cw_ref[CONV_W - 1:CONV_W, cols] * gate
        gbuf_ref[0:pad, :] = carry_ref[c]
        gbuf_ref[pad:pad + tm, :] = gate
        for j in range(CONV_W - 1):
            off = pad - hist + j * shift
            gc = gc + cw_ref[j:j + 1, cols] * gbuf_ref[off:off + tm, :]
        carry_ref[c] = gate[tm - pad:tm, :]
        act_ref[:, c * fc:(c + 1) * fc] = (jax.nn.silu(gc) * val).astype(BF16)
    acc = acc + _dot(act_ref[...], wd_ref[...])
    if interleaved:
        for t in range(tm // SUBLANES):
            o_ref[:, t, :] = acc[t * SUBLANES:(t + 1) * SUBLANES, :]
    else:
        o_ref[...] = acc


def _mix_ffn(a, resid, w_mix, g, w_up, conv_w, conv_b, w_down, tm, fc, glu, interleaved):
    bsz, seq, d = resid.shape
    m = bsz * seq
    f = w_down.shape[0]
    assert f % fc == 0
    n_chunks = f // fc
    shift = SUBLANES if interleaved else 1
    rows_per_seq = seq * shift
    tiles_per_seq = rows_per_seq // tm
    pad = max(SUBLANES, (CONV_W - 1) * shift)
    if interleaved:
        nat = pl.BlockSpec((SUBLANES, tm // SUBLANES, d),
                           lambda i: (i // tiles_per_seq, i % tiles_per_seq, 0))
        out_shape = jax.ShapeDtypeStruct((bsz, seq, d), F32)
    else:
        resid = resid.reshape(m, d)
        nat = pl.BlockSpec((tm, d), lambda i: (i, 0))
        out_shape = jax.ShapeDtypeStruct((m, d), F32)
    kern = functools.partial(_ffn_kernel, shift=shift, tiles_per_seq=tiles_per_seq, glu=glu,
                             interleaved=interleaved)
    out = pl.pallas_call(
        kern,
        grid=(m // tm,),
        in_specs=[pl.BlockSpec((tm, d), lambda i: (i, 0)), nat,
                  _resident(w_mix.shape), _resident((1, d)),
                  _resident(w_up.shape),
                  _resident((CONV_W, f)), _resident((1, f)),
                  _resident(w_down.shape)],
        out_specs=nat,
        out_shape=out_shape,
        scratch_shapes=[pltpu.VMEM((tm, d), BF16),
                        pltpu.VMEM((tm, f), BF16),
                        pltpu.VMEM((pad + tm, fc), F32),
                        pltpu.VMEM((n_chunks, pad, fc), F32)],
        compiler_params=_cparams("arbitrary"),
        name="mix_ffn",
    )(a, resid, w_mix, g.reshape(1, d), w_up, conv_w, conv_b.reshape(1, f), w_down)
    return out.reshape(bsz, seq, d)


def _head_norm(x, gain):
    lo = lax.broadcasted_iota(jnp.int32, (1, LANES), 1) < HEAD_DIM
    sq = x * x
    ss_lo = jnp.sum(jnp.where(lo, sq, 0.0), axis=-1, keepdims=True)
    ss_hi = jnp.sum(jnp.where(lo, 0.0, sq), axis=-1, keepdims=True)
    ms = jnp.where(lo, ss_lo, ss_hi) * (1.0 / HEAD_DIM)
    return x * lax.rsqrt(ms + EPS) * gain


def _qkv_kernel(h_ref, gq_ref, gkv_ref, wq_ref, wkv_ref, qg_ref, kg_ref,
                q_ref, k_ref, v_ref):
    d = h_ref.shape[1]
    h = h_ref[...]
    hn = h * _rms(h)
    q = _dot((hn * gq_ref[...]).astype(BF16), wq_ref[...])
    kv = _dot((hn * gkv_ref[...]).astype(BF16), wkv_ref[...])
    for c in range(0, d, LANES):
        q_ref[:, c:c + LANES] = _head_norm(q[:, c:c + LANES], qg_ref[...]).astype(BF16)
        k_ref[:, c:c + LANES] = _head_norm(kv[:, c:c + LANES], kg_ref[...]).astype(BF16)
    v_ref[...] = kv[:, d:].astype(BF16)


def _qkv(h, gq, gkv, wq, wkv, qg, kg, tm):
    m, d = h.shape
    row = pl.BlockSpec((tm, d), lambda i: (i, 0))
    out = jax.ShapeDtypeStruct((m, d), BF16)
    return pl.pallas_call(
        _qkv_kernel,
        grid=(m // tm,),
        in_specs=[row, _resident((1, d)), _resident((1, d)),
                  _resident(wq.shape), _resident(wkv.shape),
                  _resident((1, LANES)), _resident((1, LANES))],
        out_specs=[row, row, row],
        out_shape=[out, out, out],
        compiler_params=_cparams("parallel"),
        name="qkv_proj",
    )(h, gq.reshape(1, d), gkv.reshape(1, d), wq, wkv, qg, kg)


MASKED_LOG = -1e30


SOFTPLUS_LINEAR = 40.0


def _softplus(z):
    return jnp.maximum(z, jnp.log(1.0 + jnp.exp(jnp.minimum(z, SOFTPLUS_LINEAR))))


def _attn_kernel(q_ref, k_ref, v_ref, o_ref, *, blk):
    n_sub = q_ref.shape[1] // blk
    lo = lax.broadcasted_iota(jnp.int32, (1, LANES), 1) < HEAD_DIM
    half = blk // 2
    row = lax.broadcasted_iota(jnp.int32, (blk, blk), 0)
    col = lax.broadcasted_iota(jnp.int32, (blk, blk), 1)
    causal_half = (col < row)[:half, :half]
    keys_from = jnp.where(row >= col, 1.0, 0.0).astype(BF16)

    def rows(j):
        return slice(j * blk, (j + 1) * blk)

    def q_rows(s, hd):
        q2 = q_ref[0, rows(s), :]
        return jnp.where(lo, q2, jnp.zeros_like(q2)) if hd == 0 else jnp.where(lo, jnp.zeros_like(q2), q2)

    def split(z, diagonal):
        if not diagonal:
            return [(slice(0, blk), slice(0, blk), z)]
        first, second = slice(0, half), slice(half, blk)
        return [(first, first, jnp.where(causal_half, z[first, first], MASKED_LOG)),
                (second, first, z[second, first]),
                (second, second, jnp.where(causal_half, z[second, second], MASKED_LOG))]

    def join(parts):
        if len(parts) == 1:
            return parts[0]
        upper_left, lower_left, lower_right = parts
        return jnp.concatenate(
            [jnp.concatenate([upper_left, jnp.zeros_like(upper_left)], axis=1),
             jnp.concatenate([lower_left, lower_right], axis=1)], axis=0)

    def front(j, subtiles):
        chains = [(s, hd) for s in subtiles for hd in range(2)]
        kb = k_ref[0, rows(j), :]
        zs = [lax.dot_general(q_rows(s, hd), kb, (((1,), (1,)), ((), ())),
                              preferred_element_type=F32) for s, hd in chains]
        parts = [split(z, s == j) for z, (s, _) in zip(zs, chains)]
        sps = [join([_softplus(z).astype(BF16) for _, _, z in p]) for p in parts]
        return chains, parts, sps

    def back(j, chains, parts, sps, accs, carries):
        vb = v_ref[0, rows(j), :]
        cums = [_dot(sp, keys_from) for sp in sps]
        for c, ch in enumerate(chains):
            w = join([jnp.exp((z - cums[c][rs, cs] - carries[ch][rs, :]).astype(BF16))
                      for rs, cs, z in parts[c]])
            accs[ch] = accs[ch] + _dot(w, vb)
            carries[ch] = carries[ch] + cums[c][:, 0:1]

    accs = {(s, hd): jnp.zeros((blk, LANES), F32) for s in range(n_sub) for hd in range(2)}
    carries = {(s, hd): jnp.zeros((blk, 1), F32) for s in range(n_sub) for hd in range(2)}
    groups = []
    for j in reversed(range(n_sub)):
        subtiles = list(range(j, n_sub))
        groups += [(j, subtiles[g:g + ATTN_GROUP]) for g in range(0, len(subtiles), ATTN_GROUP)]
    pending = None
    for j, subtiles in groups:
        fronted = front(j, subtiles)
        if pending is not None:
            back(*pending, accs, carries)
        pending = (j, *fronted)
    back(*pending, accs, carries)
    for s in range(n_sub):
        o_ref[0, rows(s), :] = jnp.where(lo, accs[(s, 0)], accs[(s, 1)]).astype(BF16)


def _attention(q, k, v, blk):
    b, seq, d = q.shape
    spec = pl.BlockSpec((1, seq, LANES), lambda bi, hp: (bi, 0, hp))
    return pl.pallas_call(
        functools.partial(_attn_kernel, blk=blk),
        grid=(b, d // LANES),
        in_specs=[spec, spec, spec],
        out_specs=spec,
        out_shape=jax.ShapeDtypeStruct((b, seq, d), BF16),
        compiler_params=_cparams("parallel", "parallel"),
        name="sb_attention",
    )(q, k, v)


def _s5_tables(lam_re, lam_im, b_re, b_im, c_re, c_im, log_dt):
    n_groups, n_state = lam_re.shape
    n_ch = b_re.shape[2]
    n_blocks = n_groups // GROUP_BLOCK
    dt = jnp.exp(log_dt)[:, None]
    mag = jnp.exp(lam_re * dt)
    a_bar_re, a_bar_im = mag * jnp.cos(lam_im * dt), mag * jnp.sin(lam_im * dt)
    den = lam_re * lam_re + lam_im * lam_im
    k_re = (((a_bar_re - 1.0) * lam_re + a_bar_im * lam_im) / den)[..., None]
    k_im = ((a_bar_im * lam_re - (a_bar_re - 1.0) * lam_im) / den)[..., None]
    b_bar_re = k_re * b_re - k_im * b_im
    b_bar_im = k_re * b_im + k_im * b_re
    eye = jnp.eye(GROUP_BLOCK, dtype=F32)
    bb = jnp.stack([b_bar_re, b_bar_im]).reshape(2, n_blocks, GROUP_BLOCK, n_state, n_ch)
    bbd = jnp.einsum('rkgph,gj->kghrjp', bb, eye).reshape(
        n_blocks, GROUP_BLOCK * n_ch, 2 * GROUP_BLOCK * n_state)
    cc = jnp.stack([c_re, -c_im]).reshape(2, n_blocks, GROUP_BLOCK, n_ch, n_state)
    cbd = jnp.einsum('rkghp,gj->krgpjh', cc, eye).reshape(
        n_blocks, 2 * GROUP_BLOCK * n_state, GROUP_BLOCK * n_ch)
    a_re = jnp.broadcast_to(a_bar_re.reshape(1, -1), (SUBLANES, n_groups * n_state))
    a_im = jnp.broadcast_to(a_bar_im.reshape(1, -1), (SUBLANES, n_groups * n_state))
    return bbd.astype(BF16), cbd.astype(BF16), a_re, a_im


def kernel(x, a_norm, a_w_in, a_lam_re, a_lam_im, a_b_re, a_b_im, a_c_re, a_c_im, a_d, a_log_dt, a_w_glu, kv_norm, w_kv, k_norm, b_norm, b_w_q, b_q_norm, b_w_o, ffn_norm, ffn_w_up, ffn_conv_w, ffn_conv_b, ffn_w_down):
    bsz, seq, d = x.shape
    n_a = a_norm.shape[0]
    depth = ffn_norm.shape[0]
    m = bsz * seq
    tm = min(ROW_TILE, seq)
    s5_steps = min(S5_STEPS, seq)
    attn_blk = min(ATTN_BLOCK, seq)
    assert bsz % SUBLANES == 0 and seq % s5_steps == 0 and seq % attn_blk == 0
    assert seq % tm == 0 and d % LANES == 0

    h = x
    k = v = None
    for layer in range(depth):
        if layer < n_a:
            i = layer
            bbd, cbd, a_re, a_im = _s5_tables(a_lam_re[i], a_lam_im[i], a_b_re[i], a_b_im[i],
                                              a_c_re[i], a_c_im[i], a_log_dt[i])
            mixed = _s5_core(h, a_norm[i], a_w_in[i].astype(BF16), bbd, cbd, a_re, a_im, a_d[i],
                             s5_steps)
            w_mix, glu = a_w_glu[i].astype(BF16), True
        else:
            j = layer - n_a
            scale = HEAD_DIM ** -0.5
            qg = jnp.tile(b_q_norm[j] * scale, LANES // HEAD_DIM).reshape(1, LANES)
            kg = jnp.tile(k_norm, LANES // HEAD_DIM).reshape(1, LANES)
            q, k_new, v_new = _qkv(h.reshape(m, d), b_norm[j], kv_norm, b_w_q[j].astype(BF16),
                                   w_kv.astype(BF16), qg, kg, tm)
            if j == 0:
                k, v = k_new.reshape(bsz, seq, d), v_new.reshape(bsz, seq, d)
            mixed = _attention(q.reshape(bsz, seq, d), k, v, attn_blk).reshape(m, d)
            w_mix, glu = b_w_o[j].astype(BF16), False
        h = _mix_ffn(mixed, h, w_mix, ffn_norm[layer], ffn_w_up[layer].astype(BF16),
                     ffn_conv_w[layer], ffn_conv_b[layer], ffn_w_down[layer].astype(BF16),
                     tm, FFN_CHUNK, glu, interleaved=layer < n_a)
    return h
```

```python
import functools

import jax
import jax.numpy as jnp
from jax import lax
from jax.experimental import pallas as pl
from jax.experimental.pallas import tpu as pltpu

EPS = 1e-6
HEAD_DIM = 64
SSM_GROUP = 16
SSM_STATE = 64
CONV_W = 3
SUBLANES = 8
LANES = 128
GROUP_BLOCK = 16
ATTN_GROUP = 2
ATTN_BLOCK = 256
ROW_TILE = 512
S5_STEPS = 128
FFN_CHUNK = 256
VMEM_LIMIT = 56 * 1024 * 1024

BF16 = jnp.bfloat16
F32 = jnp.float32


def _cparams(*sem):
    return pltpu.CompilerParams(dimension_semantics=sem, vmem_limit_bytes=VMEM_LIMIT)


def _resident(shape):
    nd = len(shape)
    return pl.BlockSpec(shape, lambda *_: (0,) * nd, pipeline_mode=pl.Buffered(1))


def _rms(x):
    return lax.rsqrt(jnp.mean(x * x, axis=-1, keepdims=True) + EPS)


def _dot(a, b):
    return jnp.dot(a, b, preferred_element_type=F32)


def _interleaved_rows(x_ref):
    return jnp.concatenate([x_ref[:, t, :] for t in range(x_ref.shape[1])], axis=0)


def _s5_kernel(x_ref, g_ref, win_ref, bbd_ref, cbd_ref, are_ref, aim_ref, d_ref, o_ref,
               u_ref, s_ref, carry_ref, *, steps, n_blocks):
    half = GROUP_BLOCK * SSM_STATE
    cb = GROUP_BLOCK * SSM_GROUP
    chunks = 4

    @pl.when(pl.program_id(1) == 0)
    def _():
        carry_ref[...] = jnp.zeros_like(carry_ref)

    x = _interleaved_rows(x_ref)
    u_ref[...] = _dot((x * _rms(x) * g_ref[...]).astype(BF16), win_ref[...])

    def input_map(k):
        s_ref[k % 2] = _dot(u_ref[:, k * cb:(k + 1) * cb].astype(BF16), bbd_ref[k])

    def output_map(k):
        y = _dot(s_ref[k % 2].astype(BF16), cbd_ref[k])
        y = y + d_ref[:, k * cb:(k + 1) * cb] * u_ref[:, k * cb:(k + 1) * cb]
        o_ref[:, k * cb:(k + 1) * cb] = jax.nn.gelu(y).astype(BF16)

    def recurrence(k):
        s_k = s_ref.at[k % 2]
        for c0 in range(0, half // LANES, chunks):
            re_cols = [(c0 + c) * LANES for c in range(chunks)]
            im_cols = [half + col for col in re_cols]
            st_cols = [k * half + col for col in re_cols]
            a_re = [are_ref[:, col:col + LANES] for col in st_cols]
            a_im = [aim_ref[:, col:col + LANES] for col in st_cols]
            s_re = [carry_ref[0, :, col:col + LANES] for col in st_cols]
            s_im = [carry_ref[1, :, col:col + LANES] for col in st_cols]
            for t in range(steps):
                rows = slice(t * SUBLANES, (t + 1) * SUBLANES)
                for c in range(chunks):
                    re, im = slice(re_cols[c], re_cols[c] + LANES), slice(im_cols[c], im_cols[c] + LANES)
                    n_re = a_re[c] * s_re[c] - a_im[c] * s_im[c] + s_k[rows, re]
                    n_im = a_re[c] * s_im[c] + a_im[c] * s_re[c] + s_k[rows, im]
                    s_k[rows, re] = n_re
                    s_k[rows, im] = n_im
                    s_re[c], s_im[c] = n_re, n_im
            for c in range(chunks):
                carry_ref[0, :, st_cols[c]:st_cols[c] + LANES] = s_re[c]
                carry_ref[1, :, st_cols[c]:st_cols[c] + LANES] = s_im[c]

    input_map(0)
    if n_blocks > 1:
        input_map(1)
    for k in range(n_blocks):
        if k >= 1:
            output_map(k - 1)
            if k + 1 < n_blocks:
                input_map(k + 1)
        recurrence(k)
    output_map(n_blocks - 1)


def _s5_core(x, g, w_in, bbd, cbd, a_re, a_im, d_skip, steps):
    bsz, seq, d = x.shape
    n_blocks = bbd.shape[0]
    rows = steps * SUBLANES
    tiles = seq // steps
    n_state = a_re.shape[1]
    kern = functools.partial(_s5_kernel, steps=steps, n_blocks=n_blocks)
    return pl.pallas_call(
        kern,
        grid=(bsz // SUBLANES, tiles),
        in_specs=[pl.BlockSpec((SUBLANES, steps, d), lambda b, t: (b, t, 0)),
                  _resident((1, d)), _resident(w_in.shape),
                  _resident(bbd.shape), _resident(cbd.shape),
                  _resident(a_re.shape), _resident(a_im.shape), _resident((1, d))],
        out_specs=pl.BlockSpec((rows, d), lambda b, t: (b * tiles + t, 0)),
        out_shape=jax.ShapeDtypeStruct((bsz * seq, d), BF16),
        scratch_shapes=[pltpu.VMEM((rows, d), F32),
                        pltpu.VMEM((2, rows, 2 * GROUP_BLOCK * SSM_STATE), F32),
                        pltpu.VMEM((2, SUBLANES, n_state), F32)],
        compiler_params=_cparams("parallel", "arbitrary"),
        name="s5_core",
    )(x, g.reshape(1, d), w_in, bbd, cbd, a_re, a_im, d_skip.reshape(1, d))


def _ffn_kernel(a_ref, resid_ref, wmix_ref, g_ref, wup_ref, cw_ref, cb_ref, wd_ref, o_ref,
                xn_ref, act_ref, gbuf_ref, carry_ref, *, shift, tiles_per_seq, glu, interleaved):
    tm, d = a_ref.shape
    n_chunks, _, fc = carry_ref.shape
    f = wd_ref.shape[0]
    hist = (CONV_W - 1) * shift
    pad = gbuf_ref.shape[0] - tm

    @pl.when(pl.program_id(0) % tiles_per_seq == 0)
    def _():
        carry_ref[...] = jnp.zeros_like(carry_ref)

    a = a_ref[...]
    resid = _interleaved_rows(resid_ref) if interleaved else resid_ref[...]
    h_cols, sq = [], jnp.zeros((tm, fc), F32)
    for c0 in range(0, d, fc):
        mix = _dot(a, wmix_ref[:, c0:c0 + fc])
        if glu:
            mix = mix * jax.nn.sigmoid(_dot(a, wmix_ref[:, d + c0:d + c0 + fc]))
        h_c = resid[:, c0:c0 + fc] + mix
        sq = sq + h_c * h_c
        h_cols.append(h_c)
    h = jnp.concatenate(h_cols, axis=1)
    scale = lax.rsqrt(jnp.sum(sq, axis=-1, keepdims=True) * (1.0 / d) + EPS)
    xn_ref[...] = (h * scale * g_ref[...]).astype(BF16)
    for c in range(n_chunks):
        xn = xn_ref[...]
        val = _dot(xn, wup_ref[:, c * fc:(c + 1) * fc])
        gate = _dot(xn, wup_ref[:, f + c * fc:f + (c + 1) * fc])
        cols = pl.ds(c * fc, fc)
        gc = cb_ref[:, cols] + cw_ref[CONV_W - 1:CONV_W, cols] * gate
        gbuf_ref[0:pad, :] = carry_ref[c]
        gbuf_ref[pad:pad + tm, :] = gate
        for j in range(CONV_W - 1):
            off = pad - hist + j * shift
            gc = gc + cw_ref[j:j + 1, cols] * gbuf_ref[off:off + tm, :]
        carry_ref[c] = gate[tm - pad:tm, :]
        act_ref[:, c * fc:(c + 1) * fc] = (jax.nn.silu(gc) * val).astype(BF16)
    out = h + _dot(act_ref[...], wd_ref[...])
    if interleaved:
        for t in range(tm // SUBLANES):
            o_ref[:, t, :] = out[t * SUBLANES:(t + 1) * SUBLANES, :]
    else:
        o_ref[...] = out


def _mix_ffn(a, resid, w_mix, g, w_up, conv_w, conv_b, w_down, tm, fc, glu, interleaved):
    bsz, seq, d = resid.shape
    m = bsz * seq
    f = w_down.shape[0]
    assert f % fc == 0 and d % fc == 0
    n_chunks = f // fc
    shift = SUBLANES if interleaved else 1
    rows_per_seq = seq * shift
    tiles_per_seq = rows_per_seq // tm
    pad = max(SUBLANES, (CONV_W - 1) * shift)
    if interleaved:
        nat = pl.BlockSpec((SUBLANES, tm // SUBLANES, d),
                           lambda i: (i // tiles_per_seq, i % tiles_per_seq, 0))
        out_shape = jax.ShapeDtypeStruct((bsz, seq, d), F32)
    else:
        resid = resid.reshape(m, d)
        nat = pl.BlockSpec((tm, d), lambda i: (i, 0))
        out_shape = jax.ShapeDtypeStruct((m, d), F32)
    kern = functools.partial(_ffn_kernel, shift=shift, tiles_per_seq=tiles_per_seq, glu=glu,
                             interleaved=interleaved)
    out = pl.pallas_call(
        kern,
        grid=(m // tm,),
        in_specs=[pl.BlockSpec((tm, d), lambda i: (i, 0)), nat,
                  _resident(w_mix.shape), _resident((1, d)),
                  _resident(w_up.shape),
                  _resident((CONV_W, f)), _resident((1, f)),
                  _resident(w_down.shape)],
        out_specs=nat,
        out_shape=out_shape,
        scratch_shapes=[pltpu.VMEM((tm, d), BF16),
                        pltpu.VMEM((tm, f), BF16),
                        pltpu.VMEM((pad + tm, fc), F32),
                        pltpu.VMEM((n_chunks, pad, fc), F32)],
        compiler_params=_cparams("arbitrary"),
        name="mix_ffn",
    )(a, resid, w_mix, g.reshape(1, d), w_up, conv_w, conv_b.reshape(1, f), w_down)
    return out.reshape(bsz, seq, d)


def _head_norm(x, gain):
    lo = lax.broadcasted_iota(jnp.int32, (1, LANES), 1) < HEAD_DIM
    sq = x * x
    ss_lo = jnp.sum(jnp.where(lo, sq, 0.0), axis=-1, keepdims=True)
    ss_hi = jnp.sum(jnp.where(lo, 0.0, sq), axis=-1, keepdims=True)
    ms = jnp.where(lo, ss_lo, ss_hi) * (1.0 / HEAD_DIM)
    return x * lax.rsqrt(ms + EPS) * gain


def _qkv_kernel(h_ref, gq_ref, gkv_ref, wq_ref, wkv_ref, qg_ref, kg_ref,
                q_ref, k_ref, v_ref):
    d = h_ref.shape[1]
    h = h_ref[...]
    hn = h * _rms(h)
    q = _dot((hn * gq_ref[...]).astype(BF16), wq_ref[...])
    kv = _dot((hn * gkv_ref[...]).astype(BF16), wkv_ref[...])
    for c in range(0, d, LANES):
        q_ref[:, c:c + LANES] = _head_norm(q[:, c:c + LANES], qg_ref[...]).astype(BF16)
        k_ref[:, c:c + LANES] = _head_norm(kv[:, c:c + LANES], kg_ref[...]).astype(BF16)
    v_ref[...] = kv[:, d:].astype(BF16)


def _qkv(h, gq, gkv, wq, wkv, qg, kg, tm):
    m, d = h.shape
    row = pl.BlockSpec((tm, d), lambda i: (i, 0))
    out = jax.ShapeDtypeStruct((m, d), BF16)
    return pl.pallas_call(
        _qkv_kernel,
        grid=(m // tm,),
        in_specs=[row, _resident((1, d)), _resident((1, d)),
                  _resident(wq.shape), _resident(wkv.shape),
                  _resident((1, LANES)), _resident((1, LANES))],
        out_specs=[row, row, row],
        out_shape=[out, out, out],
        compiler_params=_cparams("parallel"),
        name="qkv_proj",
    )(h, gq.reshape(1, d), gkv.reshape(1, d), wq, wkv, qg, kg)


MASKED_LOG = -1e30


SOFTPLUS_LINEAR = 40.0


def _softplus(z):
    return jnp.maximum(z, jnp.log(1.0 + jnp.exp(jnp.minimum(z, SOFTPLUS_LINEAR))))


def _attn_kernel(q_ref, k_ref, v_ref, o_ref, *, blk):
    n_sub = q_ref.shape[1] // blk
    lo = lax.broadcasted_iota(jnp.int32, (1, LANES), 1) < HEAD_DIM
    half = blk // 2
    row = lax.broadcasted_iota(jnp.int32, (blk, blk), 0)
    col = lax.broadcasted_iota(jnp.int32, (blk, blk), 1)
    causal_half = (col < row)[:half, :half]
    keys_from = jnp.where(row >= col, 1.0, 0.0).astype(BF16)

    def rows(j):
        return slice(j * blk, (j + 1) * blk)

    def q_rows(s, hd):
        q2 = q_ref[0, rows(s), :]
        return jnp.where(lo, q2, jnp.zeros_like(q2)) if hd == 0 else jnp.where(lo, jnp.zeros_like(q2), q2)

    def split(z, diagonal):
        if not diagonal:
            return [(slice(0, blk), slice(0, blk), z)]
        first, second = slice(0, half), slice(half, blk)
        return [(first, first, jnp.where(causal_half, z[first, first], MASKED_LOG)),
                (second, first, z[second, first]),
                (second, second, jnp.where(causal_half, z[second, second], MASKED_LOG))]

    def join(parts):
        if len(parts) == 1:
            return parts[0]
        upper_left, lower_left, lower_right = parts
        return jnp.concatenate(
            [jnp.concatenate([upper_left, jnp.zeros_like(upper_left)], axis=1),
             jnp.concatenate([lower_left, lower_right], axis=1)], axis=0)

    def front(j, subtiles):
        chains = [(s, hd) for s in subtiles for hd in range(2)]
        kb = k_ref[0, rows(j), :]
        zs = [lax.dot_general(q_rows(s, hd), kb, (((1,), (1,)), ((), ())),
                              preferred_element_type=F32) for s, hd in chains]
        parts = [split(z, s == j) for z, (s, _) in zip(zs, chains)]
        sps = [join([_softplus(z).astype(BF16) for _, _, z in p]) for p in parts]
        return chains, parts, sps

    def back(j, chains, parts, sps, accs, carries):
        vb = v_ref[0, rows(j), :]
        cums = [_dot(sp, keys_from) for sp in sps]
        for c, ch in enumerate(chains):
            w = join([jnp.exp((z - cums[c][rs, cs] - carries[ch][rs, :]).astype(BF16))
                      for rs, cs, z in parts[c]])
            accs[ch] = accs[ch] + _dot(w, vb)
            carries[ch] = carries[ch] + cums[c][:, 0:1]

    accs = {(s, hd): jnp.zeros((blk, LANES), F32) for s in range(n_sub) for hd in range(2)}
    carries = {(s, hd): jnp.zeros((blk, 1), F32) for s in range(n_sub) for hd in range(2)}
    groups = []
    for j in reversed(range(n_sub)):
        subtiles = list(range(j, n_sub))
        groups += [(j, subtiles[g:g + ATTN_GROUP]) for g in range(0, len(subtiles), ATTN_GROUP)]
    pending = None
    for j, subtiles in groups:
        fronted = front(j, subtiles)
        if pending is not None:
            back(*pending, accs, carries)
        pending = (j, *fronted)
    back(*pending, accs, carries)
    for s in range(n_sub):
        o_ref[0, rows(s), :] = jnp.where(lo, accs[(s, 0)], accs[(s, 1)]).astype(BF16)


def _attention(q, k, v, blk):
    b, seq, d = q.shape
    spec = pl.BlockSpec((1, seq, LANES), lambda bi, hp: (bi, 0, hp))
    return pl.pallas_call(
        functools.partial(_attn_kernel, blk=blk),
        grid=(b, d // LANES),
        in_specs=[spec, spec, spec],
        out_specs=spec,
        out_shape=jax.ShapeDtypeStruct((b, seq, d), BF16),
        compiler_params=_cparams("parallel", "parallel"),
        name="sb_attention",
    )(q, k, v)


def _s5_tables(lam_re, lam_im, b_re, b_im, c_re, c_im, log_dt):
    n_groups, n_state = lam_re.shape
    n_ch = b_re.shape[2]
    n_blocks = n_groups // GROUP_BLOCK
    dt = jnp.exp(log_dt)[:, None]
    mag = jnp.exp(lam_re * dt)
    a_bar_re, a_bar_im = mag * jnp.cos(lam_im * dt), mag * jnp.sin(lam_im * dt)
    den = lam_re * lam_re + lam_im * lam_im
    k_re = (((a_bar_re - 1.0) * lam_re + a_bar_im * lam_im) / den)[..., None]
    k_im = ((a_bar_im * lam_re - (a_bar_re - 1.0) * lam_im) / den)[..., None]
    b_bar_re = k_re * b_re - k_im * b_im
    b_bar_im = k_re * b_im + k_im * b_re
    eye = jnp.eye(GROUP_BLOCK, dtype=F32)
    bb = jnp.stack([b_bar_re, b_bar_im]).reshape(2, n_blocks, GROUP_BLOCK, n_state, n_ch)
    bbd = jnp.einsum('rkgph,gj->kghrjp', bb, eye).reshape(
        n_blocks, GROUP_BLOCK * n_ch, 2 * GROUP_BLOCK * n_state)
    cc = jnp.stack([c_re, -c_im]).reshape(2, n_blocks, GROUP_BLOCK, n_ch, n_state)
    cbd = jnp.einsum('rkghp,gj->krgpjh', cc, eye).reshape(
        n_blocks, 2 * GROUP_BLOCK * n_state, GROUP_BLOCK * n_ch)
    a_re = jnp.broadcast_to(a_bar_re.reshape(1, -1), (SUBLANES, n_groups * n_state))
    a_im = jnp.broadcast_to(a_bar_im.reshape(1, -1), (SUBLANES, n_groups * n_state))
    return bbd.astype(BF16), cbd.astype(BF16), a_re, a_im


def kernel(x, a_norm, a_w_in, a_lam_re, a_lam_im, a_b_re, a_b_im, a_c_re, a_c_im, a_d, a_log_dt, a_w_glu, kv_norm, w_kv, k_norm, b_norm, b_w_q, b_q_norm, b_w_o, ffn_norm, ffn_w_up, ffn_conv_w, ffn_conv_b, ffn_w_down):
    bsz, seq, d = x.shape
    n_a = a_norm.shape[0]
    depth = ffn_norm.shape[0]
    m = bsz * seq
    tm = min(ROW_TILE, seq)
    s5_steps = min(S5_STEPS, seq)
    attn_blk = min(ATTN_BLOCK, seq)
    assert bsz % SUBLANES == 0 and seq % s5_steps == 0 and seq % attn_blk == 0
    assert seq % tm == 0 and d % LANES == 0
    n_groups, n_state, n_ch = a_b_re.shape[1:]
    assert (n_state, n_ch) == (SSM_STATE, SSM_GROUP) and n_groups * n_ch == d
    assert n_groups % GROUP_BLOCK == 0 and k_norm.shape == (HEAD_DIM,) and LANES % HEAD_DIM == 0
    assert ffn_conv_w.shape[1] == CONV_W

    h = x
    k = v = None
    for layer in range(depth):
        if layer < n_a:
            i = layer
            bbd, cbd, a_re, a_im = _s5_tables(a_lam_re[i], a_lam_im[i], a_b_re[i], a_b_im[i],
                                              a_c_re[i], a_c_im[i], a_log_dt[i])
            mixed = _s5_core(h, a_norm[i], a_w_in[i].astype(BF16), bbd, cbd, a_re, a_im, a_d[i],
                             s5_steps)
            w_mix, glu = a_w_glu[i].astype(BF16), True
        else:
            j = layer - n_a
            scale = HEAD_DIM ** -0.5
            qg = jnp.tile(b_q_norm[j] * scale, LANES // HEAD_DIM).reshape(1, LANES)
            kg = jnp.tile(k_norm, LANES // HEAD_DIM).reshape(1, LANES)
            q, k_new, v_new = _qkv(h.reshape(m, d), b_norm[j], kv_norm, b_w_q[j].astype(BF16),
                                   w_kv.astype(BF16), qg, kg, tm)
            if j == 0:
                k, v = k_new.reshape(bsz, seq, d), v_new.reshape(bsz, seq, d)
            mixed = _attention(q.reshape(bsz, seq, d), k, v, attn_blk).reshape(m, d)
            w_mix, glu = b_w_o[j].astype(BF16), False
        h = _mix_ffn(mixed, h, w_mix, ffn_norm[layer], ffn_w_up[layer].astype(BF16),
                     ffn_conv_w[layer], ffn_conv_b[layer], ffn_w_down[layer].astype(BF16),
                     tm, FFN_CHUNK, glu, interleaved=layer < n_a)
    return h
```

```python
import functools

import jax
import jax.numpy as jnp
from jax import lax
from jax.experimental import pallas as pl
from jax.experimental.pallas import tpu as pltpu

EPS = 1e-6
HEAD_DIM = 64
SSM_GROUP = 16
SSM_STATE = 64
CONV_W = 3
SUBLANES = 8
LANES = 128
GROUP_BLOCK = 16
ATTN_GROUP = 2
ATTN_BLOCK = 256
ROW_TILE = 512
S5_STEPS = 128
FFN_CHUNK = 256
VMEM_LIMIT = 56 * 1024 * 1024

BF16 = jnp.bfloat16
F32 = jnp.float32


def _cparams(*sem):
    return pltpu.CompilerParams(dimension_semantics=sem, vmem_limit_bytes=VMEM_LIMIT)


def _resident(shape):
    nd = len(shape)
    return pl.BlockSpec(shape, lambda *_: (0,) * nd, pipeline_mode=pl.Buffered(1))


def _rms(x):
    return lax.rsqrt(jnp.mean(x * x, axis=-1, keepdims=True) + EPS)


def _dot(a, b):
    return jnp.dot(a, b, preferred_element_type=F32)


def _interleaved_rows(x_ref):
    return jnp.concatenate([x_ref[:, t, :] for t in range(x_ref.shape[1])], axis=0)


def _s5_kernel(x_ref, g_ref, win_ref, bbd_ref, cbd_ref, are_ref, aim_ref, d_ref, o_ref,
               u_ref, s_ref, carry_ref, *, steps, n_blocks):
    half = GROUP_BLOCK * SSM_STATE
    cb = GROUP_BLOCK * SSM_GROUP
    chunks = 4

    @pl.when(pl.program_id(1) == 0)
    def _():
        carry_ref[...] = jnp.zeros_like(carry_ref)

    x = _interleaved_rows(x_ref)
    u_ref[...] = _dot((x * _rms(x) * g_ref[...]).astype(BF16), win_ref[...])

    def input_map(k):
        s_ref[k % 2] = _dot(u_ref[:, k * cb:(k + 1) * cb].astype(BF16), bbd_ref[k])

    def output_map(k):
        y = _dot(s_ref[k % 2].astype(BF16), cbd_ref[k])
        y = y + d_ref[:, k * cb:(k + 1) * cb] * u_ref[:, k * cb:(k + 1) * cb]
        o_ref[:, k * cb:(k + 1) * cb] = jax.nn.gelu(y).astype(BF16)

    def recurrence(k):
        s_k = s_ref.at[k % 2]
        for c0 in range(0, half // LANES, chunks):
            re_cols = [(c0 + c) * LANES for c in range(chunks)]
            im_cols = [half + col for col in re_cols]
            st_cols = [k * half + col for col in re_cols]
            a_re = [are_ref[:, col:col + LANES] for col in st_cols]
            a_im = [aim_ref[:, col:col + LANES] for col in st_cols]
            s_re = [carry_ref[0, :, col:col + LANES] for col in st_cols]
            s_im = [carry_ref[1, :, col:col + LANES] for col in st_cols]
            for t in range(steps):
                rows = slice(t * SUBLANES, (t + 1) * SUBLANES)
                for c in range(chunks):
                    re, im = slice(re_cols[c], re_cols[c] + LANES), slice(im_cols[c], im_cols[c] + LANES)
                    n_re = a_re[c] * s_re[c] - a_im[c] * s_im[c] + s_k[rows, re]
                    n_im = a_re[c] * s_im[c] + a_im[c] * s_re[c] + s_k[rows, im]
                    s_k[rows, re] = n_re
                    s_k[rows, im] = n_im
                    s_re[c], s_im[c] = n_re, n_im
            for c in range(chunks):
                carry_ref[0, :, st_cols[c]:st_cols[c] + LANES] = s_re[c]
                carry_ref[1, :, st_cols[c]:st_cols[c] + LANES] = s_im[c]

    input_map(0)
    if n_blocks > 1:
        input_map(1)
    for k in range(n_blocks):
        if k >= 1:
            output_map(k - 1)
            if k + 1 < n_blocks:
                input_map(k + 1)
        recurrence(k)
    output_map(n_blocks - 1)


def _s5_core(x, g, w_in, bbd, cbd, a_re, a_im, d_skip, steps):
    bsz, seq, d = x.shape
    n_blocks = bbd.shape[0]
    rows = steps * SUBLANES
    tiles = seq // steps
    n_state = a_re.shape[1]
    kern = functools.partial(_s5_kernel, steps=steps, n_blocks=n_blocks)
    return pl.pallas_call(
        kern,
        grid=(bsz // SUBLANES, tiles),
        in_specs=[pl.BlockSpec((SUBLANES, steps, d), lambda b, t: (b, t, 0)),
                  _resident((1, d)), _resident(w_in.shape),
                  _resident(bbd.shape), _resident(cbd.shape),
                  _resident(a_re.shape), _resident(a_im.shape), _resident((1, d))],
        out_specs=pl.BlockSpec((rows, d), lambda b, t: (b * tiles + t, 0)),
        out_shape=jax.ShapeDtypeStruct((bsz * seq, d), BF16),
        scratch_shapes=[pltpu.VMEM((rows, d), F32),
                        pltpu.VMEM((2, rows, 2 * GROUP_BLOCK * SSM_STATE), F32),
                        pltpu.VMEM((2, SUBLANES, n_state), F32)],
        compiler_params=_cparams("parallel", "arbitrary"),
        name="s5_core",
    )(x, g.reshape(1, d), w_in, bbd, cbd, a_re, a_im, d_skip.reshape(1, d))


def _ffn_kernel(a_ref, resid_ref, wmix_ref, g_ref, wup_ref, cw_ref, cb_ref, wd_ref, o_ref,
                xn_ref, act_ref, gbuf_ref, carry_ref, *, shift, tiles_per_seq, glu, interleaved):
    tm, d = a_ref.shape
    n_chunks, _, fc = carry_ref.shape
    f = wd_ref.shape[0]
    hist = (CONV_W - 1) * shift
    pad = gbuf_ref.shape[0] - tm

    @pl.when(pl.program_id(0) % tiles_per_seq == 0)
    def _():
        carry_ref[...] = jnp.zeros_like(carry_ref)

    a = a_ref[...]
    resid = _interleaved_rows(resid_ref) if interleaved else resid_ref[...]
    h_cols, sq = [], jnp.zeros((tm, fc), F32)
    for c0 in range(0, d, fc):
        mix = _dot(a, wmix_ref[:, c0:c0 + fc])
        if glu:
            mix = mix * jax.nn.sigmoid(_dot(a, wmix_ref[:, d + c0:d + c0 + fc]))
        h_c = resid[:, c0:c0 + fc] + mix
        sq = sq + h_c * h_c
        h_cols.append(h_c)
    h = jnp.concatenate(h_cols, axis=1)
    scale = lax.rsqrt(jnp.sum(sq, axis=-1, keepdims=True) * (1.0 / d) + EPS)
    xn_ref[...] = (h * scale * g_ref[...]).astype(BF16)
    for c in range(n_chunks):
        xn = xn_ref[...]
        val = _dot(xn, wup_ref[:, c * fc:(c + 1) * fc])
        gate = _dot(xn, wup_ref[:, f + c * fc:f + (c + 1) * fc])
        cols = pl.ds(c * fc, fc)
        gc = cb_ref[:, cols] + cw_ref[CONV_W - 1:CONV_W, cols] * gate
        gbuf_ref[0:pad, :] = carry_ref[c]
        gbuf_ref[pad:pad + tm, :] = gate
        for j in range(CONV_W - 1):
            off = pad - hist + j * shift
            gc = gc + cw_ref[j:j + 1, cols] * gbuf_ref[off:off + tm, :]
        carry_ref[c] = gate[tm - pad:tm, :]
        act_ref[:, c * fc:(c + 1) * fc] = (jax.nn.silu(gc) * val).astype(BF16)
    out = h + _dot(act_ref[...], wd_ref[...])
    if interleaved:
        for t in range(tm // SUBLANES):
            o_ref[:, t, :] = out[t * SUBLANES:(t + 1) * SUBLANES, :]
    else:
        o_ref[...] = out


def _mix_ffn(a, resid, w_mix, g, w_up, conv_w, conv_b, w_down, tm, fc, glu, interleaved):
    bsz, seq, d = resid.shape
    m = bsz * seq
    f = w_down.shape[0]
    assert f % fc == 0 and d % fc == 0
    n_chunks = f // fc
    shift = SUBLANES if interleaved else 1
    rows_per_seq = seq * shift
    tiles_per_seq = rows_per_seq // tm
    pad = max(SUBLANES, (CONV_W - 1) * shift)
    if interleaved:
        nat = pl.BlockSpec((SUBLANES, tm // SUBLANES, d),
                           lambda i: (i // tiles_per_seq, i % tiles_per_seq, 0))
        out_shape = jax.ShapeDtypeStruct((bsz, seq, d), F32)
    else:
        resid = resid.reshape(m, d)
        nat = pl.BlockSpec((tm, d), lambda i: (i, 0))
        out_shape = jax.ShapeDtypeStruct((m, d), F32)
    kern = functools.partial(_ffn_kernel, shift=shift, tiles_per_seq=tiles_per_seq, glu=glu,
                             interleaved=interleaved)
    out = pl.pallas_call(
        kern,
        grid=(m // tm,),
        in_specs=[pl.BlockSpec((tm, d), lambda i: (i, 0)), nat,
                  _resident(w_mix.shape), _resident((1, d)),
                  _resident(w_up.shape),
                  _resident((CONV_W, f)), _resident((1, f)),
                  _resident(w_down.shape)],
        out_specs=nat,
        out_shape=out_shape,
        scratch_shapes=[pltpu.VMEM((tm, d), BF16),
                        pltpu.VMEM((tm, f), BF16),
                        pltpu.VMEM((pad + tm, fc), F32),
                        pltpu.VMEM((n_chunks, pad, fc), F32)],
        compiler_params=_cparams("arbitrary"),
        name="mix_ffn",
    )(a, resid, w_mix, g.reshape(1, d), w_up, conv_w, conv_b.reshape(1, f), w_down)
    return out.reshape(bsz, seq, d)


def _head_norm(x, gain):
    lo = lax.broadcasted_iota(jnp.int32, (1, LANES), 1) < HEAD_DIM
    sq = x * x
    ss_lo = jnp.sum(jnp.where(lo, sq, 0.0), axis=-1, keepdims=True)
    ss_hi = jnp.sum(jnp.where(lo, 0.0, sq), axis=-1, keepdims=True)
    ms = jnp.where(lo, ss_lo, ss_hi) * (1.0 / HEAD_DIM)
    return x * lax.rsqrt(ms + EPS) * gain


def _qkv_kernel(h_ref, gq_ref, gkv_ref, wq_ref, wkv_ref, qg_ref, kg_ref,
                q_ref, k_ref, v_ref):
    d = h_ref.shape[1]
    h = h_ref[...]
    hn = h * _rms(h)
    q = _dot((hn * gq_ref[...]).astype(BF16), wq_ref[...])
    kv = _dot((hn * gkv_ref[...]).astype(BF16), wkv_ref[...])
    for c in range(0, d, LANES):
        q_ref[:, c:c + LANES] = _head_norm(q[:, c:c + LANES], qg_ref[...]).astype(BF16)
        k_ref[:, c:c + LANES] = _head_norm(kv[:, c:c + LANES], kg_ref[...]).astype(BF16)
    v_ref[...] = kv[:, d:].astype(BF16)


def _qkv(h, gq, gkv, wq, wkv, qg, kg, tm):
    m, d = h.shape
    row = pl.BlockSpec((tm, d), lambda i: (i, 0))
    out = jax.ShapeDtypeStruct((m, d), BF16)
    return pl.pallas_call(
        _qkv_kernel,
        grid=(m // tm,),
        in_specs=[row, _resident((1, d)), _resident((1, d)),
                  _resident(wq.shape), _resident(wkv.shape),
                  _resident((1, LANES)), _resident((1, LANES))],
        out_specs=[row, row, row],
        out_shape=[out, out, out],
        compiler_params=_cparams("parallel"),
        name="qkv_proj",
    )(h, gq.reshape(1, d), gkv.reshape(1, d), wq, wkv, qg, kg)


MASKED_LOG = -1e30


SOFTPLUS_LINEAR = 40.0


def _softplus(z):
    return jnp.maximum(z, jnp.log(1.0 + jnp.exp(jnp.minimum(z, SOFTPLUS_LINEAR))))


def _attn_kernel(q_ref, k_ref, v_ref, o_ref, *, blk):
    n_sub = q_ref.shape[1] // blk
    lo = lax.broadcasted_iota(jnp.int32, (1, LANES), 1) < HEAD_DIM
    half = blk // 2
    row = lax.broadcasted_iota(jnp.int32, (blk, blk), 0)
    col = lax.broadcasted_iota(jnp.int32, (blk, blk), 1)
    causal_half = (col < row)[:half, :half]
    keys_from = jnp.where(row >= col, 1.0, 0.0).astype(BF16)

    def rows(j):
        return slice(j * blk, (j + 1) * blk)

    def q_rows(s, hd):
        q2 = q_ref[0, rows(s), :]
        return jnp.where(lo, q2, jnp.zeros_like(q2)) if hd == 0 else jnp.where(lo, jnp.zeros_like(q2), q2)

    def split(z, diagonal):
        if not diagonal:
            return [(slice(0, blk), slice(0, blk), z)]
        first, second = slice(0, half), slice(half, blk)
        return [(first, first, jnp.where(causal_half, z[first, first], MASKED_LOG)),
                (second, first, z[second, first]),
                (second, second, jnp.where(causal_half, z[second, second], MASKED_LOG))]

    def join(parts):
        if len(parts) == 1:
            return parts[0]
        upper_left, lower_left, lower_right = parts
        return jnp.concatenate(
            [jnp.concatenate([upper_left, jnp.zeros_like(upper_left)], axis=1),
             jnp.concatenate([lower_left, lower_right], axis=1)], axis=0)

    def front(j, subtiles):
        chains = [(s, hd) for s in subtiles for hd in range(2)]
        kb = k_ref[0, rows(j), :]
        zs = [lax.dot_general(q_rows(s, hd), kb, (((1,), (1,)), ((), ())),
                              preferred_element_type=F32) for s, hd in chains]
        parts = [split(z, s == j) for z, (s, _) in zip(zs, chains)]
        sps = [join([_softplus(z).astype(BF16) for _, _, z in p]) for p in parts]
        return chains, parts, sps

    def back(j, chains, parts, sps, accs, carries):
        vb = v_ref[0, rows(j), :]
        stacked = _dot(jnp.concatenate(sps, axis=0), keys_from)
        cums = [stacked[rows(c), :] for c in range(len(chains))]
        ws = [join([jnp.exp((z - cums[c][rs, cs] - carries[ch][rs, :]).astype(BF16))
                    for rs, cs, z in parts[c]]) for c, ch in enumerate(chains)]
        weighted = _dot(jnp.concatenate(ws, axis=0), vb)
        for c, ch in enumerate(chains):
            accs[ch] = accs[ch] + weighted[rows(c), :]
            carries[ch] = carries[ch] + cums[c][:, 0:1]

    accs = {(s, hd): jnp.zeros((blk, LANES), F32) for s in range(n_sub) for hd in range(2)}
    carries = {(s, hd): jnp.zeros((blk, 1), F32) for s in range(n_sub) for hd in range(2)}
    groups = []
    for j in reversed(range(n_sub)):
        subtiles = list(range(j, n_sub))
        groups += [(j, subtiles[g:g + ATTN_GROUP]) for g in range(0, len(subtiles), ATTN_GROUP)]
    pending = None
    for j, subtiles in groups:
        fronted = front(j, subtiles)
        if pending is not None:
            back(*pending, accs, carries)
        pending = (j, *fronted)
    back(*pending, accs, carries)
    for s in range(n_sub):
        o_ref[0, rows(s), :] = jnp.where(lo, accs[(s, 0)], accs[(s, 1)]).astype(BF16)


def _attention(q, k, v, blk):
    b, seq, d = q.shape
    spec = pl.BlockSpec((1, seq, LANES), lambda bi, hp: (bi, 0, hp))
    return pl.pallas_call(
        functools.partial(_attn_kernel, blk=blk),
        grid=(b, d // LANES),
        in_specs=[spec, spec, spec],
        out_specs=spec,
        out_shape=jax.ShapeDtypeStruct((b, seq, d), BF16),
        compiler_params=_cparams("parallel", "parallel"),
        name="sb_attention",
    )(q, k, v)


def _s5_tables(lam_re, lam_im, b_re, b_im, c_re, c_im, log_dt):
    n_groups, n_state = lam_re.shape
    n_ch = b_re.shape[2]
    n_blocks = n_groups // GROUP_BLOCK
    dt = jnp.exp(log_dt)[:, None]
    mag = jnp.exp(lam_re * dt)
    a_bar_re, a_bar_im = mag * jnp.cos(lam_im * dt), mag * jnp.sin(lam_im * dt)
    den = lam_re * lam_re + lam_im * lam_im
    k_re = (((a_bar_re - 1.0) * lam_re + a_bar_im * lam_im) / den)[..., None]
    k_im = ((a_bar_im * lam_re - (a_bar_re - 1.0) * lam_im) / den)[..., None]
    b_bar_re = k_re * b_re - k_im * b_im
    b_bar_im = k_re * b_im + k_im * b_re
    eye = jnp.eye(GROUP_BLOCK, dtype=F32)
    bb = jnp.stack([b_bar_re, b_bar_im]).reshape(2, n_blocks, GROUP_BLOCK, n_state, n_ch)
    bbd = jnp.einsum('rkgph,gj->kghrjp', bb, eye).reshape(
        n_blocks, GROUP_BLOCK * n_ch, 2 * GROUP_BLOCK * n_state)
    cc = jnp.stack([c_re, -c_im]).reshape(2, n_blocks, GROUP_BLOCK, n_ch, n_state)
    cbd = jnp.einsum('rkghp,gj->krgpjh', cc, eye).reshape(
        n_blocks, 2 * GROUP_BLOCK * n_state, GROUP_BLOCK * n_ch)
    a_re = jnp.broadcast_to(a_bar_re.reshape(1, -1), (SUBLANES, n_groups * n_state))
    a_im = jnp.broadcast_to(a_bar_im.reshape(1, -1), (SUBLANES, n_groups * n_state))
    return bbd.astype(BF16), cbd.astype(BF16), a_re, a_im


def kernel(x, a_norm, a_w_in, a_lam_re, a_lam_im, a_b_re, a_b_im, a_c_re, a_c_im, a_d, a_log_dt, a_w_glu, kv_norm, w_kv, k_norm, b_norm, b_w_q, b_q_norm, b_w_o, ffn_norm, ffn_w_up, ffn_conv_w, ffn_conv_b, ffn_w_down):
    bsz, seq, d = x.shape
    n_a = a_norm.shape[0]
    depth = ffn_norm.shape[0]
    m = bsz * seq
    tm = min(ROW_TILE, seq)
    s5_steps = min(S5_STEPS, seq)
    attn_blk = min(ATTN_BLOCK, seq)
    assert bsz % SUBLANES == 0 and seq % s5_steps == 0 and seq % attn_blk == 0
    assert seq % tm == 0 and d % LANES == 0
    n_groups, n_state, n_ch = a_b_re.shape[1:]
    assert (n_state, n_ch) == (SSM_STATE, SSM_GROUP) and n_groups * n_ch == d
    assert n_groups % GROUP_BLOCK == 0 and k_norm.shape == (HEAD_DIM,) and LANES % HEAD_DIM == 0
    assert ffn_conv_w.shape[1] == CONV_W

    h = x
    k = v = None
    for layer in range(depth):
        if layer < n_a:
            i = layer
            bbd, cbd, a_re, a_im = _s5_tables(a_lam_re[i], a_lam_im[i], a_b_re[i], a_b_im[i],
                                              a_c_re[i], a_c_im[i], a_log_dt[i])
            mixed = _s5_core(h, a_norm[i], a_w_in[i].astype(BF16), bbd, cbd, a_re, a_im, a_d[i],
                             s5_steps)
            w_mix, glu = a_w_glu[i].astype(BF16), True
        else:
            j = layer - n_a
            scale = HEAD_DIM ** -0.5
            qg = jnp.tile(b_q_norm[j] * scale, LANES // HEAD_DIM).reshape(1, LANES)
            kg = jnp.tile(k_norm, LANES // HEAD_DIM).reshape(1, LANES)
            q, k_new, v_new = _qkv(h.reshape(m, d), b_norm[j], kv_norm, b_w_q[j].astype(BF16),
                                   w_kv.astype(BF16), qg, kg, tm)
            if j == 0:
                k, v = k_new.reshape(bsz, seq, d), v_new.reshape(bsz, seq, d)
            mixed = _attention(q.reshape(bsz, seq, d), k, v, attn_blk).reshape(m, d)
            w_mix, glu = b_w_o[j].astype(BF16), False
        h = _mix_ffn(mixed, h, w_mix, ffn_norm[layer], ffn_w_up[layer].astype(BF16),
                     ffn_conv_w[layer], ffn_conv_b[layer], ffn_w_down[layer].astype(BF16),
                     tm, FFN_CHUNK, glu, interleaved=layer < n_a)
    return h
```

```python
import functools

import jax
import jax.numpy as jnp
from jax import lax
from jax.experimental import pallas as pl
from jax.experimental.pallas import tpu as pltpu

EPS = 1e-6
HEAD_DIM = 64
SSM_GROUP = 16
SSM_STATE = 64
CONV_W = 3
SUBLANES = 8
LANES = 128
GROUP_BLOCK = 16
ATTN_GROUP = 2
ATTN_BLOCK = 256
ROW_TILE = 512
S5_STEPS = 128
FFN_CHUNK = 256
VMEM_LIMIT = 56 * 1024 * 1024

BF16 = jnp.bfloat16
F32 = jnp.float32


def _cparams(*sem):
    return pltpu.CompilerParams(dimension_semantics=sem, vmem_limit_bytes=VMEM_LIMIT)


def _resident(shape):
    nd = len(shape)
    return pl.BlockSpec(shape, lambda *_: (0,) * nd, pipeline_mode=pl.Buffered(1))


def _rms(x):
    return lax.rsqrt(jnp.mean(x * x, axis=-1, keepdims=True) + EPS)


def _dot(a, b):
    return jnp.dot(a, b, preferred_element_type=F32)


def _interleaved_rows(x_ref):
    return jnp.concatenate([x_ref[:, t, :] for t in range(x_ref.shape[1])], axis=0)


def _s5_kernel(x_ref, g_ref, win_ref, bbd_ref, cbd_ref, are_ref, aim_ref, d_ref, o_ref, xi_ref,
               u_ref, s_ref, carry_ref, *, steps, n_blocks):
    half = GROUP_BLOCK * SSM_STATE
    cb = GROUP_BLOCK * SSM_GROUP
    chunks = 4

    @pl.when(pl.program_id(1) == 0)
    def _():
        carry_ref[...] = jnp.zeros_like(carry_ref)

    x = _interleaved_rows(x_ref)
    xi_ref[...] = x
    u_ref[...] = _dot((x * _rms(x) * g_ref[...]).astype(BF16), win_ref[...])

    def input_map(k):
        s_ref[k % 2] = _dot(u_ref[:, k * cb:(k + 1) * cb].astype(BF16), bbd_ref[k])

    def output_map(k):
        y = _dot(s_ref[k % 2].astype(BF16), cbd_ref[k])
        y = y + d_ref[:, k * cb:(k + 1) * cb] * u_ref[:, k * cb:(k + 1) * cb]
        o_ref[:, k * cb:(k + 1) * cb] = jax.nn.gelu(y).astype(BF16)

    def recurrence(k):
        s_k = s_ref.at[k % 2]
        for c0 in range(0, half // LANES, chunks):
            re_cols = [(c0 + c) * LANES for c in range(chunks)]
            im_cols = [half + col for col in re_cols]
            st_cols = [k * half + col for col in re_cols]
            a_re = [are_ref[:, col:col + LANES] for col in st_cols]
            a_im = [aim_ref[:, col:col + LANES] for col in st_cols]
            s_re = [carry_ref[0, :, col:col + LANES] for col in st_cols]
            s_im = [carry_ref[1, :, col:col + LANES] for col in st_cols]
            for t in range(steps):
                rows = slice(t * SUBLANES, (t + 1) * SUBLANES)
                for c in range(chunks):
                    re, im = slice(re_cols[c], re_cols[c] + LANES), slice(im_cols[c], im_cols[c] + LANES)
                    n_re = a_re[c] * s_re[c] - a_im[c] * s_im[c] + s_k[rows, re]
                    n_im = a_re[c] * s_im[c] + a_im[c] * s_re[c] + s_k[rows, im]
                    s_k[rows, re] = n_re
                    s_k[rows, im] = n_im
                    s_re[c], s_im[c] = n_re, n_im
            for c in range(chunks):
                carry_ref[0, :, st_cols[c]:st_cols[c] + LANES] = s_re[c]
                carry_ref[1, :, st_cols[c]:st_cols[c] + LANES] = s_im[c]

    input_map(0)
    if n_blocks > 1:
        input_map(1)
    for k in range(n_blocks):
        if k >= 1:
            output_map(k - 1)
            if k + 1 < n_blocks:
                input_map(k + 1)
        recurrence(k)
    output_map(n_blocks - 1)


def _s5_core(x, g, w_in, bbd, cbd, a_re, a_im, d_skip, steps):
    bsz, seq, d = x.shape
    n_blocks = bbd.shape[0]
    rows = steps * SUBLANES
    tiles = seq // steps
    n_state = a_re.shape[1]
    kern = functools.partial(_s5_kernel, steps=steps, n_blocks=n_blocks)
    row_spec = pl.BlockSpec((rows, d), lambda b, t: (b * tiles + t, 0))
    return pl.pallas_call(
        kern,
        grid=(bsz // SUBLANES, tiles),
        in_specs=[pl.BlockSpec((SUBLANES, steps, d), lambda b, t: (b, t, 0)),
                  _resident((1, d)), _resident(w_in.shape),
                  _resident(bbd.shape), _resident(cbd.shape),
                  _resident(a_re.shape), _resident(a_im.shape), _resident((1, d))],
        out_specs=[row_spec, row_spec],
        out_shape=[jax.ShapeDtypeStruct((bsz * seq, d), BF16),
                   jax.ShapeDtypeStruct((bsz * seq, d), F32)],
        scratch_shapes=[pltpu.VMEM((rows, d), F32),
                        pltpu.VMEM((2, rows, 2 * GROUP_BLOCK * SSM_STATE), F32),
                        pltpu.VMEM((2, SUBLANES, n_state), F32)],
        compiler_params=_cparams("parallel", "arbitrary"),
        name="s5_core",
    )(x, g.reshape(1, d), w_in, bbd, cbd, a_re, a_im, d_skip.reshape(1, d))


def _ffn_kernel(a_ref, resid_ref, wmix_ref, g_ref, wup_ref, cw_ref, cb_ref, wd_ref, o_ref,
                xn_ref, act_ref, gbuf_ref, carry_ref, *, shift, tiles_per_seq, glu, interleaved):
    tm, d = a_ref.shape
    n_chunks, _, fc = carry_ref.shape
    f = wd_ref.shape[0]
    hist = (CONV_W - 1) * shift
    pad = gbuf_ref.shape[0] - tm

    @pl.when(pl.program_id(0) % tiles_per_seq == 0)
    def _():
        carry_ref[...] = jnp.zeros_like(carry_ref)

    a = a_ref[...]
    resid = resid_ref[...]
    h_cols, sq = [], jnp.zeros((tm, fc), F32)
    for c0 in range(0, d, fc):
        mix = _dot(a, wmix_ref[:, c0:c0 + fc])
        if glu:
            mix = mix * jax.nn.sigmoid(_dot(a, wmix_ref[:, d + c0:d + c0 + fc]))
        h_c = resid[:, c0:c0 + fc] + mix
        sq = sq + h_c * h_c
        h_cols.append(h_c)
    h = jnp.concatenate(h_cols, axis=1)
    scale = lax.rsqrt(jnp.sum(sq, axis=-1, keepdims=True) * (1.0 / d) + EPS)
    xn_ref[...] = (h * scale * g_ref[...]).astype(BF16)
    for c in range(n_chunks):
        xn = xn_ref[...]
        val = _dot(xn, wup_ref[:, c * fc:(c + 1) * fc])
        gate = _dot(xn, wup_ref[:, f + c * fc:f + (c + 1) * fc])
        cols = pl.ds(c * fc, fc)
        gc = cb_ref[:, cols] + cw_ref[CONV_W - 1:CONV_W, cols] * gate
        gbuf_ref[0:pad, :] = carry_ref[c]
        gbuf_ref[pad:pad + tm, :] = gate
        for j in range(CONV_W - 1):
            off = pad - hist + j * shift
            gc = gc + cw_ref[j:j + 1, cols] * gbuf_ref[off:off + tm, :]
        carry_ref[c] = gate[tm - pad:tm, :]
        act_ref[:, c * fc:(c + 1) * fc] = (jax.nn.silu(gc) * val).astype(BF16)
    out = h + _dot(act_ref[...], wd_ref[...])
    if interleaved:
        for t in range(tm // SUBLANES):
            o_ref[:, t, :] = out[t * SUBLANES:(t + 1) * SUBLANES, :]
    else:
        o_ref[...] = out


def _mix_ffn(a, resid, bsz, w_mix, g, w_up, conv_w, conv_b, w_down, tm, fc, glu, interleaved):
    m, d = resid.shape
    seq = m // bsz
    f = w_down.shape[0]
    assert f % fc == 0 and d % fc == 0
    n_chunks = f // fc
    shift = SUBLANES if interleaved else 1
    rows_per_seq = seq * shift
    tiles_per_seq = rows_per_seq // tm
    pad = max(SUBLANES, (CONV_W - 1) * shift)
    row = pl.BlockSpec((tm, d), lambda i: (i, 0))
    if interleaved:
        nat = pl.BlockSpec((SUBLANES, tm // SUBLANES, d),
                           lambda i: (i // tiles_per_seq, i % tiles_per_seq, 0))
        out_shape = jax.ShapeDtypeStruct((bsz, seq, d), F32)
    else:
        nat = row
        out_shape = jax.ShapeDtypeStruct((m, d), F32)
    kern = functools.partial(_ffn_kernel, shift=shift, tiles_per_seq=tiles_per_seq, glu=glu,
                             interleaved=interleaved)
    out = pl.pallas_call(
        kern,
        grid=(m // tm,),
        in_specs=[row, row,
                  _resident(w_mix.shape), _resident((1, d)),
                  _resident(w_up.shape),
                  _resident((CONV_W, f)), _resident((1, f)),
                  _resident(w_down.shape)],
        out_specs=nat,
        out_shape=out_shape,
        scratch_shapes=[pltpu.VMEM((tm, d), BF16),
                        pltpu.VMEM((tm, f), BF16),
                        pltpu.VMEM((pad + tm, fc), F32),
                        pltpu.VMEM((n_chunks, pad, fc), F32)],
        compiler_params=_cparams("arbitrary"),
        name="mix_ffn",
    )(a, resid, w_mix, g.reshape(1, d), w_up, conv_w, conv_b.reshape(1, f), w_down)
    return out.reshape(bsz, seq, d)


def _head_norm(x, gain):
    lo = lax.broadcasted_iota(jnp.int32, (1, LANES), 1) < HEAD_DIM
    sq = x * x
    ss_lo = jnp.sum(jnp.where(lo, sq, 0.0), axis=-1, keepdims=True)
    ss_hi = jnp.sum(jnp.where(lo, 0.0, sq), axis=-1, keepdims=True)
    ms = jnp.where(lo, ss_lo, ss_hi) * (1.0 / HEAD_DIM)
    return x * lax.rsqrt(ms + EPS) * gain


def _qkv_kernel(h_ref, gq_ref, gkv_ref, wq_ref, wkv_ref, qg_ref, kg_ref,
                q_ref, k_ref, v_ref):
    d = h_ref.shape[1]
    h = h_ref[...]
    hn = h * _rms(h)
    q = _dot((hn * gq_ref[...]).astype(BF16), wq_ref[...])
    kv = _dot((hn * gkv_ref[...]).astype(BF16), wkv_ref[...])
    for c in range(0, d, LANES):
        q_ref[:, c:c + LANES] = _head_norm(q[:, c:c + LANES], qg_ref[...]).astype(BF16)
        k_ref[:, c:c + LANES] = _head_norm(kv[:, c:c + LANES], kg_ref[...]).astype(BF16)
    v_ref[...] = kv[:, d:].astype(BF16)


def _qkv(h, gq, gkv, wq, wkv, qg, kg, tm):
    m, d = h.shape
    row = pl.BlockSpec((tm, d), lambda i: (i, 0))
    out = jax.ShapeDtypeStruct((m, d), BF16)
    return pl.pallas_call(
        _qkv_kernel,
        grid=(m // tm,),
        in_specs=[row, _resident((1, d)), _resident((1, d)),
                  _resident(wq.shape), _resident(wkv.shape),
                  _resident((1, LANES)), _resident((1, LANES))],
        out_specs=[row, row, row],
        out_shape=[out, out, out],
        compiler_params=_cparams("parallel"),
        name="qkv_proj",
    )(h, gq.reshape(1, d), gkv.reshape(1, d), wq, wkv, qg, kg)


MASKED_LOG = -1e30


SOFTPLUS_LINEAR = 40.0


def _softplus(z):
    return jnp.maximum(z, jnp.log(1.0 + jnp.exp(jnp.minimum(z, SOFTPLUS_LINEAR))))


def _attn_kernel(q_ref, k_ref, v_ref, o_ref, *, blk):
    n_sub = q_ref.shape[1] // blk
    lo = lax.broadcasted_iota(jnp.int32, (1, LANES), 1) < HEAD_DIM
    half = blk // 2
    row = lax.broadcasted_iota(jnp.int32, (blk, blk), 0)
    col = lax.broadcasted_iota(jnp.int32, (blk, blk), 1)
    causal_half = (col < row)[:half, :half]
    keys_from = jnp.where(row >= col, 1.0, 0.0).astype(BF16)

    def rows(j):
        return slice(j * blk, (j + 1) * blk)

    def q_rows(s, hd):
        q2 = q_ref[0, rows(s), :]
        return jnp.where(lo, q2, jnp.zeros_like(q2)) if hd == 0 else jnp.where(lo, jnp.zeros_like(q2), q2)

    def split(z, diagonal):
        if not diagonal:
            return [(slice(0, blk), slice(0, blk), z)]
        first, second = slice(0, half), slice(half, blk)
        return [(first, first, jnp.where(causal_half, z[first, first], MASKED_LOG)),
                (second, first, z[second, first]),
                (second, second, jnp.where(causal_half, z[second, second], MASKED_LOG))]

    def join(parts):
        if len(parts) == 1:
            return parts[0]
        upper_left, lower_left, lower_right = parts
        return jnp.concatenate(
            [jnp.concatenate([upper_left, jnp.zeros_like(upper_left)], axis=1),
             jnp.concatenate([lower_left, lower_right], axis=1)], axis=0)

    def front(j, subtiles):
        chains = [(s, hd) for s in subtiles for hd in range(2)]
        kb = k_ref[0, rows(j), :]
        zs = [lax.dot_general(q_rows(s, hd), kb, (((1,), (1,)), ((), ())),
                              preferred_element_type=F32) for s, hd in chains]
        parts = [split(z, s == j) for z, (s, _) in zip(zs, chains)]
        sps = [join([_softplus(z).astype(BF16) for _, _, z in p]) for p in parts]
        return chains, parts, sps

    def back(j, chains, parts, sps, accs, carries):
        vb = v_ref[0, rows(j), :]
        stacked = _dot(jnp.concatenate(sps, axis=0), keys_from)
        cums = [stacked[rows(c), :] for c in range(len(chains))]
        ws = [join([jnp.exp((z - cums[c][rs, cs] - carries[ch][rs, :]).astype(BF16))
                    for rs, cs, z in parts[c]]) for c, ch in enumerate(chains)]
        weighted = _dot(jnp.concatenate(ws, axis=0), vb)
        for c, ch in enumerate(chains):
            accs[ch] = accs[ch] + weighted[rows(c), :]
            carries[ch] = carries[ch] + cums[c][:, 0:1]

    accs = {(s, hd): jnp.zeros((blk, LANES), F32) for s in range(n_sub) for hd in range(2)}
    carries = {(s, hd): jnp.zeros((blk, 1), F32) for s in range(n_sub) for hd in range(2)}
    groups = []
    for j in reversed(range(n_sub)):
        subtiles = list(range(j, n_sub))
        groups += [(j, subtiles[g:g + ATTN_GROUP]) for g in range(0, len(subtiles), ATTN_GROUP)]
    pending = None
    for j, subtiles in groups:
        fronted = front(j, subtiles)
        if pending is not None:
            back(*pending, accs, carries)
        pending = (j, *fronted)
    back(*pending, accs, carries)
    for s in range(n_sub):
        o_ref[0, rows(s), :] = jnp.where(lo, accs[(s, 0)], accs[(s, 1)]).astype(BF16)


def _attention(q, k, v, blk):
    b, seq, d = q.shape
    spec = pl.BlockSpec((1, seq, LANES), lambda bi, hp: (bi, 0, hp))
    return pl.pallas_call(
        functools.partial(_attn_kernel, blk=blk),
        grid=(b, d // LANES),
        in_specs=[spec, spec, spec],
        out_specs=spec,
        out_shape=jax.ShapeDtypeStruct((b, seq, d), BF16),
        compiler_params=_cparams("parallel", "parallel"),
        name="sb_attention",
    )(q, k, v)


def _s5_tables(lam_re, lam_im, b_re, b_im, c_re, c_im, log_dt):
    n_groups, n_state = lam_re.shape
    n_ch = b_re.shape[2]
    n_blocks = n_groups // GROUP_BLOCK
    dt = jnp.exp(log_dt)[:, None]
    mag = jnp.exp(lam_re * dt)
    a_bar_re, a_bar_im = mag * jnp.cos(lam_im * dt), mag * jnp.sin(lam_im * dt)
    den = lam_re * lam_re + lam_im * lam_im
    k_re = (((a_bar_re - 1.0) * lam_re + a_bar_im * lam_im) / den)[..., None]
    k_im = ((a_bar_im * lam_re - (a_bar_re - 1.0) * lam_im) / den)[..., None]
    b_bar_re = k_re * b_re - k_im * b_im
    b_bar_im = k_re * b_im + k_im * b_re
    eye = jnp.eye(GROUP_BLOCK, dtype=F32)
    bb = jnp.stack([b_bar_re, b_bar_im]).reshape(2, n_blocks, GROUP_BLOCK, n_state, n_ch)
    bbd = jnp.einsum('rkgph,gj->kghrjp', bb, eye).reshape(
        n_blocks, GROUP_BLOCK * n_ch, 2 * GROUP_BLOCK * n_state)
    cc = jnp.stack([c_re, -c_im]).reshape(2, n_blocks, GROUP_BLOCK, n_ch, n_state)
    cbd = jnp.einsum('rkghp,gj->krgpjh', cc, eye).reshape(
        n_blocks, 2 * GROUP_BLOCK * n_state, GROUP_BLOCK * n_ch)
    a_re = jnp.broadcast_to(a_bar_re.reshape(1, -1), (SUBLANES, n_groups * n_state))
    a_im = jnp.broadcast_to(a_bar_im.reshape(1, -1), (SUBLANES, n_groups * n_state))
    return bbd.astype(BF16), cbd.astype(BF16), a_re, a_im


def kernel(x, a_norm, a_w_in, a_lam_re, a_lam_im, a_b_re, a_b_im, a_c_re, a_c_im, a_d, a_log_dt, a_w_glu, kv_norm, w_kv, k_norm, b_norm, b_w_q, b_q_norm, b_w_o, ffn_norm, ffn_w_up, ffn_conv_w, ffn_conv_b, ffn_w_down):
    bsz, seq, d = x.shape
    n_a = a_norm.shape[0]
    depth = ffn_norm.shape[0]
    m = bsz * seq
    tm = min(ROW_TILE, seq)
    s5_steps = min(S5_STEPS, seq)
    attn_blk = min(ATTN_BLOCK, seq)
    assert bsz % SUBLANES == 0 and seq % s5_steps == 0 and seq % attn_blk == 0
    assert seq % tm == 0 and d % LANES == 0
    n_groups, n_state, n_ch = a_b_re.shape[1:]
    assert (n_state, n_ch) == (SSM_STATE, SSM_GROUP) and n_groups * n_ch == d
    assert n_groups % GROUP_BLOCK == 0 and k_norm.shape == (HEAD_DIM,) and LANES % HEAD_DIM == 0
    assert ffn_conv_w.shape[1] == CONV_W

    h = x
    k = v = None
    for layer in range(depth):
        if layer < n_a:
            i = layer
            bbd, cbd, a_re, a_im = _s5_tables(a_lam_re[i], a_lam_im[i], a_b_re[i], a_b_im[i],
                                              a_c_re[i], a_c_im[i], a_log_dt[i])
            mixed, resid = _s5_core(h, a_norm[i], a_w_in[i].astype(BF16), bbd, cbd, a_re, a_im,
                                    a_d[i], s5_steps)
            w_mix, glu = a_w_glu[i].astype(BF16), True
        else:
            j = layer - n_a
            scale = HEAD_DIM ** -0.5
            qg = jnp.tile(b_q_norm[j] * scale, LANES // HEAD_DIM).reshape(1, LANES)
            kg = jnp.tile(k_norm, LANES // HEAD_DIM).reshape(1, LANES)
            q, k_new, v_new = _qkv(h.reshape(m, d), b_norm[j], kv_norm, b_w_q[j].astype(BF16),
                                   w_kv.astype(BF16), qg, kg, tm)
            if j == 0:
                k, v = k_new.reshape(bsz, seq, d), v_new.reshape(bsz, seq, d)
            mixed = _attention(q.reshape(bsz, seq, d), k, v, attn_blk).reshape(m, d)
            resid = h.reshape(m, d)
            w_mix, glu = b_w_o[j].astype(BF16), False
        h = _mix_ffn(mixed, resid, bsz, w_mix, ffn_norm[layer], ffn_w_up[layer].astype(BF16),
                     ffn_conv_w[layer], ffn_conv_b[layer], ffn_w_down[layer].astype(BF16),
                     tm, FFN_CHUNK, glu, interleaved=layer < n_a)
    return h
```

```python
import functools

import jax
import jax.numpy as jnp
from jax import lax
from jax.experimental import pallas as pl
from jax.experimental.pallas import tpu as pltpu

EPS = 1e-6
HEAD_DIM = 64
SSM_GROUP = 16
SSM_STATE = 64
CONV_W = 3
SUBLANES = 8
LANES = 128
GROUP_BLOCK = 8
ATTN_GROUP = 2
ATTN_BLOCK = 256
ROW_TILE = 512
S5_STEPS = 128
FFN_CHUNK = 256
VMEM_LIMIT = 56 * 1024 * 1024

BF16 = jnp.bfloat16
F32 = jnp.float32


def _cparams(*sem):
    return pltpu.CompilerParams(dimension_semantics=sem, vmem_limit_bytes=VMEM_LIMIT)


def _resident(shape):
    nd = len(shape)
    return pl.BlockSpec(shape, lambda *_: (0,) * nd, pipeline_mode=pl.Buffered(1))


def _rms(x):
    return lax.rsqrt(jnp.mean(x * x, axis=-1, keepdims=True) + EPS)


def _dot(a, b):
    return jnp.dot(a, b, preferred_element_type=F32)


def _interleaved_rows(x_ref):
    return jnp.concatenate([x_ref[:, t, :] for t in range(x_ref.shape[1])], axis=0)


def _s5_kernel(x_ref, g_ref, win_ref, bbd_ref, cbd_ref, are_ref, aim_ref, d_ref, o_ref, xi_ref,
               u_ref, s_ref, carry_ref, *, steps, n_blocks):
    half = GROUP_BLOCK * SSM_STATE
    cb = GROUP_BLOCK * SSM_GROUP
    chunks = 4

    @pl.when(pl.program_id(1) == 0)
    def _():
        carry_ref[...] = jnp.zeros_like(carry_ref)

    x = _interleaved_rows(x_ref)
    xi_ref[...] = x
    u_ref[...] = _dot((x * _rms(x) * g_ref[...]).astype(BF16), win_ref[...])

    def input_map(k):
        s_ref[k % 2] = _dot(u_ref[:, k * cb:(k + 1) * cb].astype(BF16), bbd_ref[k])

    def output_map(k):
        y = _dot(s_ref[k % 2].astype(BF16), cbd_ref[k])
        y = y + d_ref[:, k * cb:(k + 1) * cb] * u_ref[:, k * cb:(k + 1) * cb]
        o_ref[:, k * cb:(k + 1) * cb] = jax.nn.gelu(y).astype(BF16)

    def recurrence(k):
        s_k = s_ref.at[k % 2]
        for c0 in range(0, half // LANES, chunks):
            re_cols = [(c0 + c) * LANES for c in range(chunks)]
            im_cols = [half + col for col in re_cols]
            st_cols = [k * half + col for col in re_cols]
            a_re = [are_ref[:, col:col + LANES] for col in st_cols]
            a_im = [aim_ref[:, col:col + LANES] for col in st_cols]
            s_re = [carry_ref[0, :, col:col + LANES] for col in st_cols]
            s_im = [carry_ref[1, :, col:col + LANES] for col in st_cols]
            for t in range(steps):
                rows = slice(t * SUBLANES, (t + 1) * SUBLANES)
                for c in range(chunks):
                    re, im = slice(re_cols[c], re_cols[c] + LANES), slice(im_cols[c], im_cols[c] + LANES)
                    n_re = a_re[c] * s_re[c] - a_im[c] * s_im[c] + s_k[rows, re]
                    n_im = a_re[c] * s_im[c] + a_im[c] * s_re[c] + s_k[rows, im]
                    s_k[rows, re] = n_re
                    s_k[rows, im] = n_im
                    s_re[c], s_im[c] = n_re, n_im
            for c in range(chunks):
                carry_ref[0, :, st_cols[c]:st_cols[c] + LANES] = s_re[c]
                carry_ref[1, :, st_cols[c]:st_cols[c] + LANES] = s_im[c]

    input_map(0)
    if n_blocks > 1:
        input_map(1)
    for k in range(n_blocks):
        if k >= 1:
            output_map(k - 1)
            if k + 1 < n_blocks:
                input_map(k + 1)
        recurrence(k)
    output_map(n_blocks - 1)


def _s5_core(x, g, w_in, bbd, cbd, a_re, a_im, d_skip, steps):
    bsz, seq, d = x.shape
    n_blocks = bbd.shape[0]
    rows = steps * SUBLANES
    tiles = seq // steps
    n_state = a_re.shape[1]
    kern = functools.partial(_s5_kernel, steps=steps, n_blocks=n_blocks)
    row_spec = pl.BlockSpec((rows, d), lambda b, t: (b * tiles + t, 0))
    return pl.pallas_call(
        kern,
        grid=(bsz // SUBLANES, tiles),
        in_specs=[pl.BlockSpec((SUBLANES, steps, d), lambda b, t: (b, t, 0)),
                  _resident((1, d)), _resident(w_in.shape),
                  _resident(bbd.shape), _resident(cbd.shape),
                  _resident(a_re.shape), _resident(a_im.shape), _resident((1, d))],
        out_specs=[row_spec, row_spec],
        out_shape=[jax.ShapeDtypeStruct((bsz * seq, d), BF16),
                   jax.ShapeDtypeStruct((bsz * seq, d), F32)],
        scratch_shapes=[pltpu.VMEM((rows, d), F32),
                        pltpu.VMEM((2, rows, 2 * GROUP_BLOCK * SSM_STATE), F32),
                        pltpu.VMEM((2, SUBLANES, n_state), F32)],
        compiler_params=_cparams("parallel", "arbitrary"),
        name="s5_core",
    )(x, g.reshape(1, d), w_in, bbd, cbd, a_re, a_im, d_skip.reshape(1, d))


def _ffn_kernel(a_ref, resid_ref, wmix_ref, g_ref, wup_ref, cw_ref, cb_ref, wd_ref, o_ref,
                xn_ref, act_ref, gbuf_ref, carry_ref, *, shift, tiles_per_seq, glu, interleaved):
    tm, d = a_ref.shape
    n_chunks, _, fc = carry_ref.shape
    f = wd_ref.shape[0]
    hist = (CONV_W - 1) * shift
    pad = gbuf_ref.shape[0] - tm

    @pl.when(pl.program_id(0) % tiles_per_seq == 0)
    def _():
        carry_ref[...] = jnp.zeros_like(carry_ref)

    a = a_ref[...]
    resid = resid_ref[...]
    h_cols, sq = [], jnp.zeros((tm, fc), F32)
    for c0 in range(0, d, fc):
        mix = _dot(a, wmix_ref[:, c0:c0 + fc])
        if glu:
            mix = mix * jax.nn.sigmoid(_dot(a, wmix_ref[:, d + c0:d + c0 + fc]))
        h_c = resid[:, c0:c0 + fc] + mix
        sq = sq + h_c * h_c
        h_cols.append(h_c)
    h = jnp.concatenate(h_cols, axis=1)
    scale = lax.rsqrt(jnp.sum(sq, axis=-1, keepdims=True) * (1.0 / d) + EPS)
    xn_ref[...] = (h * scale * g_ref[...]).astype(BF16)
    for c in range(n_chunks):
        xn = xn_ref[...]
        val = _dot(xn, wup_ref[:, c * fc:(c + 1) * fc])
        gate = _dot(xn, wup_ref[:, f + c * fc:f + (c + 1) * fc])
        cols = pl.ds(c * fc, fc)
        gc = cb_ref[:, cols] + cw_ref[CONV_W - 1:CONV_W, cols] * gate
        gbuf_ref[0:pad, :] = carry_ref[c]
        gbuf_ref[pad:pad + tm, :] = gate
        for j in range(CONV_W - 1):
            off = pad - hist + j * shift
            gc = gc + cw_ref[j:j + 1, cols] * gbuf_ref[off:off + tm, :]
        carry_ref[c] = gate[tm - pad:tm, :]
        act_ref[:, c * fc:(c + 1) * fc] = (jax.nn.silu(gc) * val).astype(BF16)
    out = h + _dot(act_ref[...], wd_ref[...])
    if interleaved:
        for t in range(tm // SUBLANES):
            o_ref[:, t, :] = out[t * SUBLANES:(t + 1) * SUBLANES, :]
    else:
        o_ref[...] = out


def _mix_ffn(a, resid, bsz, w_mix, g, w_up, conv_w, conv_b, w_down, tm, fc, glu, interleaved):
    m, d = resid.shape
    seq = m // bsz
    f = w_down.shape[0]
    assert f % fc == 0 and d % fc == 0
    n_chunks = f // fc
    shift = SUBLANES if interleaved else 1
    rows_per_seq = seq * shift
    tiles_per_seq = rows_per_seq // tm
    pad = max(SUBLANES, (CONV_W - 1) * shift)
    row = pl.BlockSpec((tm, d), lambda i: (i, 0))
    if interleaved:
        nat = pl.BlockSpec((SUBLANES, tm // SUBLANES, d),
                           lambda i: (i // tiles_per_seq, i % tiles_per_seq, 0))
        out_shape = jax.ShapeDtypeStruct((bsz, seq, d), F32)
    else:
        nat = row
        out_shape = jax.ShapeDtypeStruct((m, d), F32)
    kern = functools.partial(_ffn_kernel, shift=shift, tiles_per_seq=tiles_per_seq, glu=glu,
                             interleaved=interleaved)
    out = pl.pallas_call(
        kern,
        grid=(m // tm,),
        in_specs=[row, row,
                  _resident(w_mix.shape), _resident((1, d)),
                  _resident(w_up.shape),
                  _resident((CONV_W, f)), _resident((1, f)),
                  _resident(w_down.shape)],
        out_specs=nat,
        out_shape=out_shape,
        scratch_shapes=[pltpu.VMEM((tm, d), BF16),
                        pltpu.VMEM((tm, f), BF16),
                        pltpu.VMEM((pad + tm, fc), F32),
                        pltpu.VMEM((n_chunks, pad, fc), F32)],
        compiler_params=_cparams("arbitrary"),
        name="mix_ffn",
    )(a, resid, w_mix, g.reshape(1, d), w_up, conv_w, conv_b.reshape(1, f), w_down)
    return out.reshape(bsz, seq, d)


def _head_norm(x, gain):
    lo = lax.broadcasted_iota(jnp.int32, (1, LANES), 1) < HEAD_DIM
    sq = x * x
    ss_lo = jnp.sum(jnp.where(lo, sq, 0.0), axis=-1, keepdims=True)
    ss_hi = jnp.sum(jnp.where(lo, 0.0, sq), axis=-1, keepdims=True)
    ms = jnp.where(lo, ss_lo, ss_hi) * (1.0 / HEAD_DIM)
    return x * lax.rsqrt(ms + EPS) * gain


def _qkv_kernel(h_ref, gq_ref, gkv_ref, wq_ref, wkv_ref, qg_ref, kg_ref,
                q_ref, k_ref, v_ref):
    d = h_ref.shape[1]
    h = h_ref[...]
    hn = h * _rms(h)
    q = _dot((hn * gq_ref[...]).astype(BF16), wq_ref[...])
    kv = _dot((hn * gkv_ref[...]).astype(BF16), wkv_ref[...])
    for c in range(0, d, LANES):
        q_ref[:, c:c + LANES] = _head_norm(q[:, c:c + LANES], qg_ref[...]).astype(BF16)
        k_ref[:, c:c + LANES] = _head_norm(kv[:, c:c + LANES], kg_ref[...]).astype(BF16)
    v_ref[...] = kv[:, d:].astype(BF16)


def _qkv(h, gq, gkv, wq, wkv, qg, kg, tm):
    m, d = h.shape
    row = pl.BlockSpec((tm, d), lambda i: (i, 0))
    out = jax.ShapeDtypeStruct((m, d), BF16)
    return pl.pallas_call(
        _qkv_kernel,
        grid=(m // tm,),
        in_specs=[row, _resident((1, d)), _resident((1, d)),
                  _resident(wq.shape), _resident(wkv.shape),
                  _resident((1, LANES)), _resident((1, LANES))],
        out_specs=[row, row, row],
        out_shape=[out, out, out],
        compiler_params=_cparams("parallel"),
        name="qkv_proj",
    )(h, gq.reshape(1, d), gkv.reshape(1, d), wq, wkv, qg, kg)


MASKED_LOG = -1e30


SOFTPLUS_LINEAR = 40.0


def _softplus(z):
    return jnp.maximum(z, jnp.log(1.0 + jnp.exp(jnp.minimum(z, SOFTPLUS_LINEAR))))


def _attn_kernel(q_ref, k_ref, v_ref, o_ref, *, blk):
    n_sub = q_ref.shape[1] // blk
    lo = lax.broadcasted_iota(jnp.int32, (1, LANES), 1) < HEAD_DIM
    half = blk // 2
    row = lax.broadcasted_iota(jnp.int32, (blk, blk), 0)
    col = lax.broadcasted_iota(jnp.int32, (blk, blk), 1)
    causal_half = (col < row)[:half, :half]
    keys_from = jnp.where(row >= col, 1.0, 0.0).astype(BF16)

    def rows(j):
        return slice(j * blk, (j + 1) * blk)

    def q_rows(s, hd):
        q2 = q_ref[0, rows(s), :]
        return jnp.where(lo, q2, jnp.zeros_like(q2)) if hd == 0 else jnp.where(lo, jnp.zeros_like(q2), q2)

    def split(z, diagonal):
        if not diagonal:
            return [(slice(0, blk), slice(0, blk), z)]
        first, second = slice(0, half), slice(half, blk)
        return [(first, first, jnp.where(causal_half, z[first, first], MASKED_LOG)),
                (second, first, z[second, first]),
                (second, second, jnp.where(causal_half, z[second, second], MASKED_LOG))]

    def join(parts):
        if len(parts) == 1:
            return parts[0]
        upper_left, lower_left, lower_right = parts
        return jnp.concatenate(
            [jnp.concatenate([upper_left, jnp.zeros_like(upper_left)], axis=1),
             jnp.concatenate([lower_left, lower_right], axis=1)], axis=0)

    def front(j, subtiles):
        chains = [(s, hd) for s in subtiles for hd in range(2)]
        kb = k_ref[0, rows(j), :]
        zs = [lax.dot_general(q_rows(s, hd), kb, (((1,), (1,)), ((), ())),
                              preferred_element_type=F32) for s, hd in chains]
        parts = [split(z, s == j) for z, (s, _) in zip(zs, chains)]
        sps = [join([_softplus(z).astype(BF16) for _, _, z in p]) for p in parts]
        return chains, parts, sps

    def back(j, chains, parts, sps, accs, carries):
        vb = v_ref[0, rows(j), :]
        stacked = _dot(jnp.concatenate(sps, axis=0), keys_from)
        cums = [stacked[rows(c), :] for c in range(len(chains))]
        ws = [join([jnp.exp((z - cums[c][rs, cs] - carries[ch][rs, :]).astype(BF16))
                    for rs, cs, z in parts[c]]) for c, ch in enumerate(chains)]
        weighted = _dot(jnp.concatenate(ws, axis=0), vb)
        for c, ch in enumerate(chains):
            accs[ch] = accs[ch] + weighted[rows(c), :]
            carries[ch] = carries[ch] + cums[c][:, 0:1]

    accs = {(s, hd): jnp.zeros((blk, LANES), F32) for s in range(n_sub) for hd in range(2)}
    carries = {(s, hd): jnp.zeros((blk, 1), F32) for s in range(n_sub) for hd in range(2)}
    groups = []
    for j in reversed(range(n_sub)):
        subtiles = list(range(j, n_sub))
        groups += [(j, subtiles[g:g + ATTN_GROUP]) for g in range(0, len(subtiles), ATTN_GROUP)]
    pending = None
    for j, subtiles in groups:
        fronted = front(j, subtiles)
        if pending is not None:
            back(*pending, accs, carries)
        pending = (j, *fronted)
    back(*pending, accs, carries)
    for s in range(n_sub):
        o_ref[0, rows(s), :] = jnp.where(lo, accs[(s, 0)], accs[(s, 1)]).astype(BF16)


def _attention(q, k, v, blk):
    b, seq, d = q.shape
    spec = pl.BlockSpec((1, seq, LANES), lambda bi, hp: (bi, 0, hp))
    return pl.pallas_call(
        functools.partial(_attn_kernel, blk=blk),
        grid=(b, d // LANES),
        in_specs=[spec, spec, spec],
        out_specs=spec,
        out_shape=jax.ShapeDtypeStruct((b, seq, d), BF16),
        compiler_params=_cparams("parallel", "parallel"),
        name="sb_attention",
    )(q, k, v)


def _s5_tables(lam_re, lam_im, b_re, b_im, c_re, c_im, log_dt):
    n_groups, n_state = lam_re.shape
    n_ch = b_re.shape[2]
    n_blocks = n_groups // GROUP_BLOCK
    dt = jnp.exp(log_dt)[:, None]
    mag = jnp.exp(lam_re * dt)
    a_bar_re, a_bar_im = mag * jnp.cos(lam_im * dt), mag * jnp.sin(lam_im * dt)
    den = lam_re * lam_re + lam_im * lam_im
    k_re = (((a_bar_re - 1.0) * lam_re + a_bar_im * lam_im) / den)[..., None]
    k_im = ((a_bar_im * lam_re - (a_bar_re - 1.0) * lam_im) / den)[..., None]
    b_bar_re = k_re * b_re - k_im * b_im
    b_bar_im = k_re * b_im + k_im * b_re
    eye = jnp.eye(GROUP_BLOCK, dtype=F32)
    bb = jnp.stack([b_bar_re, b_bar_im]).reshape(2, n_blocks, GROUP_BLOCK, n_state, n_ch)
    bbd = jnp.einsum('rkgph,gj->kghrjp', bb, eye).reshape(
        n_blocks, GROUP_BLOCK * n_ch, 2 * GROUP_BLOCK * n_state)
    cc = jnp.stack([c_re, -c_im]).reshape(2, n_blocks, GROUP_BLOCK, n_ch, n_state)
    cbd = jnp.einsum('rkghp,gj->krgpjh', cc, eye).reshape(
        n_blocks, 2 * GROUP_BLOCK * n_state, GROUP_BLOCK * n_ch)
    a_re = jnp.broadcast_to(a_bar_re.reshape(1, -1), (SUBLANES, n_groups * n_state))
    a_im = jnp.broadcast_to(a_bar_im.reshape(1, -1), (SUBLANES, n_groups * n_state))
    return bbd.astype(BF16), cbd.astype(BF16), a_re, a_im


def kernel(x, a_norm, a_w_in, a_lam_re, a_lam_im, a_b_re, a_b_im, a_c_re, a_c_im, a_d, a_log_dt, a_w_glu, kv_norm, w_kv, k_norm, b_norm, b_w_q, b_q_norm, b_w_o, ffn_norm, ffn_w_up, ffn_conv_w, ffn_conv_b, ffn_w_down):
    bsz, seq, d = x.shape
    n_a = a_norm.shape[0]
    depth = ffn_norm.shape[0]
    m = bsz * seq
    tm = min(ROW_TILE, seq)
    s5_steps = min(S5_STEPS, seq)
    attn_blk = min(ATTN_BLOCK, seq)
    assert bsz % SUBLANES == 0 and seq % s5_steps == 0 and seq % attn_blk == 0
    assert seq % tm == 0 and d % LANES == 0
    n_groups, n_state, n_ch = a_b_re.shape[1:]
    assert (n_state, n_ch) == (SSM_STATE, SSM_GROUP) and n_groups * n_ch == d
    assert n_groups % GROUP_BLOCK == 0 and k_norm.shape == (HEAD_DIM,) and LANES % HEAD_DIM == 0
    assert ffn_conv_w.shape[1] == CONV_W

    h = x
    k = v = None
    for layer in range(depth):
        if layer < n_a:
            i = layer
            bbd, cbd, a_re, a_im = _s5_tables(a_lam_re[i], a_lam_im[i], a_b_re[i], a_b_im[i],
                                              a_c_re[i], a_c_im[i], a_log_dt[i])
            mixed, resid = _s5_core(h, a_norm[i], a_w_in[i].astype(BF16), bbd, cbd, a_re, a_im,
                                    a_d[i], s5_steps)
            w_mix, glu = a_w_glu[i].astype(BF16), True
        else:
            j = layer - n_a
            scale = HEAD_DIM ** -0.5
            qg = jnp.tile(b_q_norm[j] * scale, LANES // HEAD_DIM).reshape(1, LANES)
            kg = jnp.tile(k_norm, LANES // HEAD_DIM).reshape(1, LANES)
            q, k_new, v_new = _qkv(h.reshape(m, d), b_norm[j], kv_norm, b_w_q[j].astype(BF16),
                                   w_kv.astype(BF16), qg, kg, tm)
            if j == 0:
                k, v = k_new.reshape(bsz, seq, d), v_new.reshape(bsz, seq, d)
            mixed = _attention(q.reshape(bsz, seq, d), k, v, attn_blk).reshape(m, d)
            resid = h.reshape(m, d)
            w_mix, glu = b_w_o[j].astype(BF16), False
        h = _mix_ffn(mixed, resid, bsz, w_mix, ffn_norm[layer], ffn_w_up[layer].astype(BF16),
                     ffn_conv_w[layer], ffn_conv_b[layer], ffn_w_down[layer].astype(BF16),
                     tm, FFN_CHUNK, glu, interleaved=layer < n_a)
    return h
```
